```python
import jax, jax.numpy as jnp
from jax import lax
import numpy as np

D_MODEL = 2048
BATCH = 2
SEQ = 16384
DEPTH = 2
DEC_BATCH = 4
DEC_SEQ = 2048
PAST_LEN = 128

GRID_W = 64
NA_HEADS = 16
NA_HEAD_DIM = 64
D_ATTN = NA_HEADS * NA_HEAD_DIM
NA_KH_MAX = 8
NA_KW = 16
D_CONV = 1024
CONV_K = 31
D_FF = 5504
FFN_CONV_K = 3
N_BRANCH = 2
D_IN = 3 * D_ATTN + 2 * D_CONV + N_BRANCH * D_MODEL
N_MOD = 6
LN_EPS = 1e-5
DN_ALPHA = (2 * DEPTH) ** 0.25
DN_BETA = (8 * DEPTH) ** -0.25
NEG_INF = -1e9

kernel_name = 'hybrid_na_conformer_deepnorm_encoder'


def layer_norm(x, g, b):
    xf = x.astype(jnp.float32)
    mu = jnp.mean(xf, axis=-1, keepdims=True)
    var = jnp.mean(jnp.square(xf - mu), axis=-1, keepdims=True)
    y = ((xf - mu) * lax.rsqrt(var + LN_EPS)).astype(x.dtype)
    return y * g + b


def depthwise_conv(x, w, b):
    k = w.shape[0]
    y = lax.conv_general_dilated(
        x, w[:, None, :], window_strides=(1,), padding=[(k // 2, k // 2)],
        dimension_numbers=('NWC', 'WIO', 'NWC'), feature_group_count=x.shape[-1])
    return y + b


def neighbourhood_attention(q, k, v, rpb):
    B, T, H, Dh = q.shape
    rows = T // GRID_W
    kh = min(NA_KH_MAX, rows)
    qg = (q * (Dh ** -0.5)).reshape(B, rows, GRID_W, H, Dh)
    kg = k.reshape(B, rows, GRID_W, H, Dh)
    vg = v.reshape(B, rows, GRID_W, H, Dh)
    cols = jnp.arange(GRID_W)
    col_start = jnp.clip(cols - NA_KW // 2, 0, GRID_W - NA_KW)
    col_mask = (cols[None, :] >= col_start[:, None]) & (cols[None, :] < col_start[:, None] + NA_KW)
    dc_idx = jnp.clip(cols[None, :] - cols[:, None] + NA_KW - 1, 0, 2 * NA_KW - 2)
    rpb_cols = rpb[:, :, dc_idx]
    mask = col_mask[None, None, :, None, :]

    def one_row(r):
        r0 = jnp.clip(r - kh // 2, 0, rows - kh)
        k_blk = lax.dynamic_slice_in_dim(kg, r0, kh, axis=1)
        v_blk = lax.dynamic_slice_in_dim(vg, r0, kh, axis=1)
        q_row = lax.dynamic_index_in_dim(qg, r, axis=1, keepdims=False)
        dr_idx = r0 + jnp.arange(kh) - r + NA_KH_MAX - 1
        bias = jnp.take(rpb_cols, dr_idx, axis=1).transpose(0, 2, 1, 3)
        s = jnp.einsum('bqhd,brkhd->bhqrk', q_row, k_blk).astype(jnp.float32) + bias.astype(jnp.float32)
        s = jnp.where(mask, s, NEG_INF)
        p = jax.nn.softmax(s.reshape(B, H, GRID_W, kh * GRID_W), axis=-1)
        p = p.reshape(B, H, GRID_W, kh, GRID_W).astype(v.dtype)
        return jnp.einsum('bhqrk,brkhd->bqhd', p, v_blk)

    out = lax.map(one_row, jnp.arange(rows))
    return out.transpose(1, 0, 2, 3, 4).reshape(B, T, H * Dh)


def encoder_layer(x, c, w_mod, b_mod, w_in, b_in, na_rpb, w_attn_proj, conv_w, conv_b, conv_ln_g, conv_ln_b,
                  w_conv_proj, w_out, ln_mix_g, ln_mix_b, w_up, ffn_conv_w, ffn_conv_b, w_down,
                  ln_ffn_g, ln_ffn_b):
    B, T, D = x.shape
    mod = jnp.einsum('bd,de->be', jax.nn.silu(c), w_mod) + b_mod
    sh1, sc1, g1, sh2, sc2, g2 = jnp.split(mod[:, None, :], N_MOD, axis=-1)

    h = x * (1 + sc1) + sh1
    z = jnp.einsum('btd,de->bte', h, w_in) + b_in
    q, k, v, u, gates = jnp.split(
        z, [D_ATTN, 2 * D_ATTN, 3 * D_ATTN, 3 * D_ATTN + 2 * D_CONV], axis=-1)
    attn = neighbourhood_attention(q.reshape(B, T, NA_HEADS, NA_HEAD_DIM),
                                   k.reshape(B, T, NA_HEADS, NA_HEAD_DIM),
                                   v.reshape(B, T, NA_HEADS, NA_HEAD_DIM), na_rpb)
    y_a = jnp.einsum('bte,ed->btd', attn, w_attn_proj)
    u_val, u_gate = jnp.split(u, 2, axis=-1)
    u = u_val * jax.nn.sigmoid(u_gate)
    u = layer_norm(depthwise_conv(u, conv_w, conv_b), conv_ln_g, conv_ln_b)
    y_c = jnp.einsum('bte,ed->btd', jax.nn.silu(u), w_conv_proj)
    g_a, g_c = jnp.split(gates, N_BRANCH, axis=-1)
    y = jax.nn.sigmoid(g_a) * y_a + jax.nn.sigmoid(g_c) * y_c
    y = jnp.einsum('btd,de->bte', y, w_out)
    x = layer_norm(DN_ALPHA * x + g1 * y, ln_mix_g, ln_mix_b)

    h = x * (1 + sc2) + sh2
    up = jnp.einsum('btd,df->btf', h, w_up)
    a, b = jnp.split(up, 2, axis=-1)
    a = depthwise_conv(a, ffn_conv_w, ffn_conv_b)
    f = jnp.einsum('btf,fd->btd', jax.nn.gelu(a, approximate=False) * b, w_down)
    x = layer_norm(DN_ALPHA * x + g2 * f, ln_ffn_g, ln_ffn_b)
    return x


def trunk(x, c, ln_in_g, ln_in_b, w_mod, b_mod, w_in, b_in, na_rpb, w_attn_proj, conv_w, conv_b,
          conv_ln_g, conv_ln_b, w_conv_proj, w_out, ln_mix_g, ln_mix_b, w_up, ffn_conv_w, ffn_conv_b,
          w_down, ln_ffn_g, ln_ffn_b):
    x = layer_norm(x, ln_in_g, ln_in_b)
    for l in range(DEPTH):
        x = encoder_layer(x, c, w_mod[l], b_mod[l], w_in[l], b_in[l], na_rpb[l], w_attn_proj[l],
                          conv_w[l], conv_b[l], conv_ln_g[l], conv_ln_b[l], w_conv_proj[l], w_out[l],
                          ln_mix_g[l], ln_mix_b[l], w_up[l], ffn_conv_w[l], ffn_conv_b[l], w_down[l],
                          ln_ffn_g[l], ln_ffn_b[l])
    return x


def setup_inputs(seed: int = 0) -> dict:
    key = jax.random.key(seed)
    ks = jax.random.split(key, 26)
    L, D = DEPTH, D_MODEL

    def nrm(k, shape, scale):
        return jax.random.normal(k, shape, jnp.float32) * scale

    return {
        'x_prompt': nrm(ks[0], (BATCH, SEQ, D), 1.0),
        'x_sample': nrm(ks[1], (DEC_BATCH, DEC_SEQ, D), 1.0),
        'c_prompt': nrm(ks[2], (BATCH, D), 1.0),
        'c_sample': nrm(ks[3], (DEC_BATCH, D), 1.0),
        'ln_in_g': 1.0 + nrm(ks[4], (D,), 0.02),
        'ln_in_b': nrm(ks[5], (D,), 0.02),
        'w_mod': nrm(ks[6], (L, D, N_MOD * D), D ** -0.5),
        'b_mod': nrm(ks[7], (L, N_MOD * D), 0.02),
        'w_in': nrm(ks[8], (L, D, D_IN), D ** -0.5),
        'b_in': nrm(ks[9], (L, D_IN), 0.02),
        'na_rpb': nrm(ks[10], (L, NA_HEADS, 2 * NA_KH_MAX - 1, 2 * NA_KW - 1), 0.1),
        'w_attn_proj': nrm(ks[11], (L, D_ATTN, D), D_ATTN ** -0.5),
        'conv_w': nrm(ks[12], (L, CONV_K, D_CONV), CONV_K ** -0.5),
        'conv_b': nrm(ks[13], (L, D_CONV), 0.02),
        'conv_ln_g': 1.0 + nrm(ks[14], (L, D_CONV), 0.02),
        'conv_ln_b': nrm(ks[15], (L, D_CONV), 0.02),
        'w_conv_proj': nrm(ks[16], (L, D_CONV, D), D_CONV ** -0.5),
        'w_out': nrm(ks[17], (L, D, D), DN_BETA * D ** -0.5),
        'ln_mix_g': 1.0 + nrm(ks[18], (L, D), 0.02),
        'ln_mix_b': nrm(ks[19], (L, D), 0.02),
        'w_up': nrm(ks[20], (L, D, 2 * D_FF), D ** -0.5),
        'ffn_conv_w': nrm(ks[21], (L, FFN_CONV_K, D_FF), FFN_CONV_K ** -0.5),
        'ffn_conv_b': nrm(ks[22], (L, D_FF), 0.02),
        'w_down': nrm(ks[23], (L, D_FF, D), DN_BETA * D_FF ** -0.5),
        'ln_ffn_g': 1.0 + nrm(ks[24], (L, D), 0.02),
        'ln_ffn_b': nrm(ks[25], (L, D), 0.02),
    }


def reference(x_prompt, x_sample, c_prompt, c_sample, ln_in_g, ln_in_b, w_mod, b_mod, w_in, b_in, na_rpb,
              w_attn_proj, conv_w, conv_b, conv_ln_g, conv_ln_b, w_conv_proj, w_out, ln_mix_g, ln_mix_b,
              w_up, ffn_conv_w, ffn_conv_b, w_down, ln_ffn_g, ln_ffn_b):
    y_prompt = trunk(x_prompt, c_prompt, ln_in_g, ln_in_b, w_mod, b_mod, w_in, b_in, na_rpb, w_attn_proj,
                     conv_w, conv_b, conv_ln_g, conv_ln_b, w_conv_proj, w_out, ln_mix_g, ln_mix_b,
                     w_up, ffn_conv_w, ffn_conv_b, w_down, ln_ffn_g, ln_ffn_b)
    y_sample = trunk(x_sample, c_sample, ln_in_g, ln_in_b, w_mod, b_mod, w_in, b_in, na_rpb, w_attn_proj,
                     conv_w, conv_b, conv_ln_g, conv_ln_b, w_conv_proj, w_out, ln_mix_g, ln_mix_b,
                     w_up, ffn_conv_w, ffn_conv_b, w_down, ln_ffn_g, ln_ffn_b)
    return (y_prompt, y_sample)
```

```python
import functools
import math

import jax
import jax.numpy as jnp
from jax import lax
from jax.experimental import pallas as pl
from jax.experimental.pallas import tpu as pltpu

GRID_W = 64
NA_HEADS = 16
NA_HEAD_DIM = 64
NA_KH_MAX = 8
NA_KW = 16
CONV_K = 31
FFN_CONV_K = 3
N_MOD = 6
LN_EPS = 1e-5
NEG_INF = -1e9

LANES = 128
HALO = 16
VMEM_LIMIT = 56 * 1024 * 1024

F32 = jnp.float32
BF16 = jnp.bfloat16


def _tile(n, pref, mult):
    t = min(pref, n)
    t -= t % mult
    while t > mult and n % t:
        t -= mult
    assert t >= mult and n % t == 0, (n, pref, mult)
    return t


def _ln(x, g, b):
    mu = jnp.mean(x, axis=-1, keepdims=True)
    xc = x - mu
    var = jnp.mean(xc * xc, axis=-1, keepdims=True)
    return xc * lax.rsqrt(var + LN_EPS) * g + b


def _params(*sem):
    return pltpu.CompilerParams(dimension_semantics=sem, vmem_limit_bytes=VMEM_LIMIT)


def _resident(shape):
    return pl.BlockSpec(shape, lambda *_: (0,) * len(shape), pipeline_mode=pl.Buffered(1))


def _mod_kernel(c_ref, w_ref, b_ref, o_ref):
    c = c_ref[...]
    a = c * jax.nn.sigmoid(c)
    o_ref[...] = jnp.dot(a, w_ref[...], preferred_element_type=F32) + b_ref[...]


def _modulation(c_all, w_mod, b_mod):
    n_layers, d, e = w_mod.shape
    r = c_all.shape[0]
    tn = _tile(e, 1024, LANES)
    return pl.pallas_call(
        _mod_kernel,
        grid=(n_layers, e // tn),
        in_specs=[
            pl.BlockSpec((r, d), lambda l, j: (0, 0)),
            pl.BlockSpec((None, d, tn), lambda l, j: (l, 0, j)),
            pl.BlockSpec((None, 1, tn), lambda l, j: (l, 0, j)),
        ],
        out_specs=pl.BlockSpec((None, r, tn), lambda l, j: (l, 0, j)),
        out_shape=jax.ShapeDtypeStruct((n_layers, r, e), F32),
        compiler_params=_params("arbitrary", "arbitrary"),
        name="modulation",
    )(c_all, w_mod, b_mod)


def _inproj_kernel(*refs, first_layer, n_qkv_tiles):
    if first_layer:
        x_ref, mod_ref, g_ref, b_ref, w_ref, bias_ref, x0_ref, qkv_ref, zr_ref, h_scr = refs
    else:
        x_ref, mod_ref, w_ref, bias_ref, qkv_ref, zr_ref, h_scr = refs
    j = pl.program_id(1)

    @pl.when(j == 0)
    def _():
        x = x_ref[...]
        if first_layer:
            x = _ln(x, g_ref[...], b_ref[...])
            x0_ref[...] = x
        h_scr[...] = (x * (1.0 + mod_ref[1:2, :]) + mod_ref[0:1, :]).astype(BF16)

    res = jnp.dot(h_scr[...], w_ref[...], preferred_element_type=F32) + bias_ref[...]

    @pl.when(j < n_qkv_tiles)
    def _():
        for p in range(qkv_ref.shape[0]):
            qkv_ref[p] = res[:, p * LANES:(p + 1) * LANES].astype(BF16)

    @pl.when(j >= n_qkv_tiles)
    def _():
        zr_ref[...] = res.astype(BF16)


def _in_projection(x, mod, w_in, b_in, seq, d_attn, ln_in=None):
    n, d = x.shape
    e = w_in.shape[1]
    tm = _tile(seq, 512, HALO)
    tn = _tile(math.gcd(3 * d_attn, e - 3 * d_attn), 1024, LANES)
    nq = 3 * d_attn // tn
    first = ln_in is not None
    row = lambda i, j: (i, 0)
    in_specs = [pl.BlockSpec((tm, d), row),
                pl.BlockSpec((None, N_MOD, d), lambda i, j: (i * tm // seq, 0, 0))]
    args = [x, mod]
    if first:
        in_specs += [pl.BlockSpec((1, d), lambda i, j: (0, 0))] * 2
        args += list(ln_in)
    in_specs += [pl.BlockSpec((d, tn), lambda i, j: (0, j)),
                 pl.BlockSpec((1, tn), lambda i, j: (0, j))]
    args += [w_in, b_in]
    out_specs = [pl.BlockSpec((tn // LANES, tm, LANES), lambda i, j: (jnp.minimum(j, nq - 1), i, 0)),
                 pl.BlockSpec((tm, tn), lambda i, j: (i, jnp.maximum(j - nq, 0)))]
    out_shape = [jax.ShapeDtypeStruct((3 * d_attn // LANES, n, LANES), BF16),
                 jax.ShapeDtypeStruct((n, e - 3 * d_attn), BF16)]
    if first:
        out_specs = [pl.BlockSpec((tm, d), row)] + out_specs
        out_shape = [jax.ShapeDtypeStruct((n, d), F32)] + out_shape
    return pl.pallas_call(
        functools.partial(_inproj_kernel, first_layer=first, n_qkv_tiles=nq),
        grid=(n // tm, e // tn),
        in_specs=in_specs,
        out_specs=out_specs,
        out_shape=out_shape,
        scratch_shapes=[pltpu.VMEM((tm, d), BF16)],
        compiler_params=_params("arbitrary", "arbitrary"),
        name="in_projection",
    )(*args)


def _attn_kernel(q_ref, k_ref, v_ref, bias_ref, o_ref, *, rows, rows_per_step):
    kh = NA_KH_MAX
    nk = kh * GRID_W
    step = pl.program_id(2)
    lane = lax.broadcasted_iota(jnp.int32, (GRID_W, LANES), 1)
    low_half = lane < NA_HEAD_DIM
    qc = lax.broadcasted_iota(jnp.int32, (GRID_W, nk), 0)
    kc = lax.broadcasted_iota(jnp.int32, (GRID_W, nk), 1) & (GRID_W - 1)
    c0 = jnp.clip(qc - NA_KW // 2, 0, GRID_W - NA_KW)
    mask = (kc >= c0) & (kc < c0 + NA_KW)
    scale = NA_HEAD_DIM ** -0.5

    def one_row(rr, carry):
        r = step * rows_per_step + rr
        r0 = jnp.clip(r - kh // 2, 0, rows - kh)
        shift = r0 - r + (NA_KH_MAX - 1)
        koff = pl.multiple_of(r0 * GRID_W, GRID_W)
        qoff = pl.multiple_of(rr * GRID_W, GRID_W)
        k_win = k_ref[pl.ds(koff, nk), :]
        v_win = v_ref[pl.ds(koff, nk), :]
        q = q_ref[pl.ds(qoff, GRID_W), :]
        outs = []
        for half in range(2):
            keep = low_half if half == 0 else jnp.logical_not(low_half)
            qh = jnp.where(keep, q, jnp.zeros_like(q))
            s = lax.dot_general(qh, k_win, (((1,), (1,)), ((), ())), preferred_element_type=F32)
            s = s * scale + bias_ref[shift, half]
            s = jnp.where(mask, s, NEG_INF)
            m = jnp.max(s, axis=-1, keepdims=True)
            p = jnp.exp(s - m)
            denom = jnp.sum(p, axis=-1, keepdims=True)
            pv = jnp.dot(p.astype(BF16), v_win, preferred_element_type=F32)
            outs.append(pv / denom)
        o_ref[pl.ds(qoff, GRID_W), :] = jnp.where(low_half, outs[0], outs[1]).astype(o_ref.dtype)
        return carry

    lax.fori_loop(0, rows_per_step, one_row, 0)


def _bias_table(rpb):
    heads = rpb.shape[0]
    cols = jnp.arange(GRID_W)
    dc = jnp.clip(cols[None, :] - cols[:, None] + NA_KW - 1, 0, 2 * NA_KW - 2)
    by_col = rpb[:, :, dc]
    dr = jnp.arange(NA_KH_MAX)[:, None] + jnp.arange(NA_KH_MAX)[None, :]
    t = by_col[:, dr]
    t = t.transpose(0, 1, 3, 2, 4).reshape(heads, NA_KH_MAX, GRID_W, NA_KH_MAX * GRID_W)
    return t.reshape(heads // 2, 2, NA_KH_MAX, GRID_W, NA_KH_MAX * GRID_W).transpose(0, 2, 1, 3, 4)


def _attention(qkv, bias, batch, seq):
    planes, n, _ = qkv.shape
    pairs = planes // 3
    rows = seq // GRID_W
    assert seq % GRID_W == 0 and rows >= NA_KH_MAX
    rps = _tile(rows, 8, 1)
    tq = rps * GRID_W
    steps = rows // rps
    return pl.pallas_call(
        functools.partial(_attn_kernel, rows=rows, rows_per_step=rps),
        grid=(batch, pairs, steps),
        in_specs=[
            pl.BlockSpec((None, tq, LANES), lambda b, p, s: (p, b * steps + s, 0)),
            pl.BlockSpec((None, seq, LANES), lambda b, p, s: (pairs + p, b, 0)),
            pl.BlockSpec((None, seq, LANES), lambda b, p, s: (2 * pairs + p, b, 0)),
            pl.BlockSpec((None,) + bias.shape[1:], lambda b, p, s: (p, 0, 0, 0, 0)),
        ],
        out_specs=pl.BlockSpec((tq, LANES), lambda b, p, s: (b * steps + s, p)),
        out_shape=jax.ShapeDtypeStruct((n, pairs * LANES), BF16),
        compiler_params=_params("arbitrary", "arbitrary", "arbitrary"),
        name="neighbourhood_attention",
    )(qkv, qkv, qkv, bias)


def _conv_kernel(uv_ref, ug_ref, pv_ref, pg_ref, nv_ref, ng_ref, w_ref, cb_ref, g_ref, b_ref, o_ref, buf,
                 *, tiles_per_seq):
    tt = uv_ref.shape[0]
    t = pl.program_id(0) % tiles_per_seq

    def glu(v_ref, gate_ref):
        return v_ref[...].astype(F32) * jax.nn.sigmoid(gate_ref[...].astype(F32))

    buf[0:HALO, :] = jnp.where(t == 0, 0.0, glu(pv_ref, pg_ref))
    buf[HALO:HALO + tt, :] = glu(uv_ref, ug_ref)
    buf[HALO + tt:, :] = jnp.where(t == tiles_per_seq - 1, 0.0, glu(nv_ref, ng_ref))
    acc = jnp.zeros(o_ref.shape, F32) + cb_ref[...]
    for k in range(CONV_K):
        start = HALO - CONV_K // 2 + k
        acc = acc + w_ref[k:k + 1, :] * buf[start:start + tt, :]
    y = _ln(acc, g_ref[...], b_ref[...])
    o_ref[...] = (y * jax.nn.sigmoid(y)).astype(o_ref.dtype)


def _conv_branch(zr, conv_w, conv_b, ln_g, ln_b, seq):
    n = zr.shape[0]
    c = conv_w.shape[1]
    tt = _tile(seq, 512, HALO)
    th = tt // HALO
    last_halo = n // HALO - 1
    prev = lambda col: (lambda i: (jnp.maximum(i * th - 1, 0), col))
    nxt = lambda col: (lambda i: (jnp.minimum((i + 1) * th, last_halo), col))
    vec = pl.BlockSpec((1, c), lambda i: (0, 0))
    return pl.pallas_call(
        functools.partial(_conv_kernel, tiles_per_seq=seq // tt),
        grid=(n // tt,),
        in_specs=[
            pl.BlockSpec((tt, c), lambda i: (i, 0)),
            pl.BlockSpec((tt, c), lambda i: (i, 1)),
            pl.BlockSpec((HALO, c), prev(0)),
            pl.BlockSpec((HALO, c), prev(1)),
            pl.BlockSpec((HALO, c), nxt(0)),
            pl.BlockSpec((HALO, c), nxt(1)),
            pl.BlockSpec((CONV_K, c), lambda i: (0, 0)),
            vec, vec, vec,
        ],
        out_specs=pl.BlockSpec((tt, c), lambda i: (i, 0)),
        out_shape=jax.ShapeDtypeStruct((n, c), BF16),
        scratch_shapes=[pltpu.VMEM((tt + 2 * HALO, c), F32)],
        compiler_params=_params("arbitrary"),
        name="conv_branch",
    )(zr, zr, zr, zr, zr, zr, conv_w, conv_b, ln_g, ln_b)


def _mix_kernel(attn_ref, cu_ref, ga_ref, gc_ref, x_ref, mod_ref, wa_ref, wc_ref, wo_ref, g_ref, b_ref,
                x1_ref, h2_ref, *, alpha):
    y_a = jnp.dot(attn_ref[...], wa_ref[...], preferred_element_type=F32)
    y_c = jnp.dot(cu_ref[...], wc_ref[...], preferred_element_type=F32)
    y = jax.nn.sigmoid(ga_ref[...].astype(F32)) * y_a + jax.nn.sigmoid(gc_ref[...].astype(F32)) * y_c
    y = jnp.dot(y.astype(BF16), wo_ref[...], preferred_element_type=F32)
    x1 = _ln(alpha * x_ref[...] + mod_ref[2:3, :] * y, g_ref[...], b_ref[...])
    x1_ref[...] = x1
    h2_ref[...] = (x1 * (1.0 + mod_ref[4:5, :]) + mod_ref[3:4, :]).astype(BF16)


def _mix(attn, cu, zr, x, mod, w_attn_proj, w_conv_proj, w_out, ln_g, ln_b, seq, alpha):
    n, d = x.shape
    c = cu.shape[1]
    da = attn.shape[1]
    assert (2 * c) % d == 0
    gate_blk = 2 * c // d
    tm = _tile(seq, 256, HALO)
    row = lambda i: (i, 0)
    return pl.pallas_call(
        functools.partial(_mix_kernel, alpha=alpha),
        grid=(n // tm,),
        in_specs=[
            pl.BlockSpec((tm, da), row),
            pl.BlockSpec((tm, c), row),
            pl.BlockSpec((tm, d), lambda i: (i, gate_blk)),
            pl.BlockSpec((tm, d), lambda i: (i, gate_blk + 1)),
            pl.BlockSpec((tm, d), row),
            pl.BlockSpec((None, N_MOD, d), lambda i: (i * tm // seq, 0, 0)),
            _resident(w_attn_proj.shape),
            _resident(w_conv_proj.shape),
            _resident(w_out.shape),
            _resident(ln_g.shape),
            _resident(ln_b.shape),
        ],
        out_specs=[pl.BlockSpec((tm, d), row), pl.BlockSpec((tm, d), row)],
        out_shape=[jax.ShapeDtypeStruct((n, d), F32), jax.ShapeDtypeStruct((n, d), BF16)],
        compiler_params=_params("arbitrary"),
        name="merge_out_projection",
    )(attn, cu, zr, zr, x, mod, w_attn_proj, w_conv_proj, w_out, ln_g, ln_b)


def _ffn_kernel(h_ref, hp_ref, hn_ref, x_ref, mod_ref, wa_ref, wb_ref, cw_ref, cb_ref, wd_ref, g_ref, b_ref,
                o_ref, h_ext, a_ext, acc, *, tiles_per_seq, alpha):
    tm = h_ref.shape[0]
    j = pl.program_id(1)
    t = pl.program_id(0) % tiles_per_seq

    @pl.when(j == 0)
    def _():
        h_ext[0:HALO, :] = jnp.where(t == 0, jnp.zeros_like(hp_ref), hp_ref[...])
        h_ext[HALO:HALO + tm, :] = h_ref[...]
        h_ext[HALO + tm:, :] = jnp.where(t == tiles_per_seq - 1, jnp.zeros_like(hn_ref), hn_ref[...])
        acc[...] = jnp.zeros_like(acc)

    a_ext[...] = jnp.dot(h_ext[...], wa_ref[...], preferred_element_type=F32)
    gate = jnp.dot(h_ref[...], wb_ref[...], preferred_element_type=F32)
    a = cb_ref[...]
    for k in range(FFN_CONV_K):
        start = HALO - FFN_CONV_K // 2 + k
        a = a + cw_ref[k:k + 1, :] * a_ext[start:start + tm, :]
    act = 0.5 * a * (1.0 + lax.erf(a * (2.0 ** -0.5))) * gate
    acc[...] += jnp.dot(act.astype(BF16), wd_ref[...], preferred_element_type=F32)

    @pl.when(j == pl.num_programs(1) - 1)
    def _():
        o_ref[...] = _ln(alpha * x_ref[...] + mod_ref[5:6, :] * acc[...], g_ref[...], b_ref[...])


def _ffn(h2, x1, mod, w_up, conv_w, conv_b, w_down, ln_g, ln_b, seq, alpha):
    n, d = x1.shape
    fp = w_down.shape[0]
    tm = _tile(seq, 512, HALO)
    tf = _tile(fp, 512, LANES)
    nf = fp // tf
    th = tm // HALO
    last_halo = n // HALO - 1
    row = lambda i, j: (i, 0)
    vec = pl.BlockSpec((1, d), lambda i, j: (0, 0))
    return pl.pallas_call(
        functools.partial(_ffn_kernel, tiles_per_seq=seq // tm, alpha=alpha),
        grid=(n // tm, nf),
        in_specs=[
            pl.BlockSpec((tm, d), row),
            pl.BlockSpec((HALO, d), lambda i, j: (jnp.maximum(i * th - 1, 0), 0)),
            pl.BlockSpec((HALO, d), lambda i, j: (jnp.minimum((i + 1) * th, last_halo), 0)),
            pl.BlockSpec((tm, d), row),
            pl.BlockSpec((None, N_MOD, d), lambda i, j: (i * tm // seq, 0, 0)),
            pl.BlockSpec((d, tf), lambda i, j: (0, j)),
            pl.BlockSpec((d, tf), lambda i, j: (0, nf + j)),
            pl.BlockSpec((FFN_CONV_K, tf), lambda i, j: (0, j)),
            pl.BlockSpec((1, tf), lambda i, j: (0, j)),
            pl.BlockSpec((tf, d), lambda i, j: (j, 0)),
            vec, vec,
        ],
        out_specs=pl.BlockSpec((tm, d), row),
        out_shape=jax.ShapeDtypeStruct((n, d), F32),
        scratch_shapes=[pltpu.VMEM((tm + 2 * HALO, d), BF16),
                        pltpu.VMEM((tm + 2 * HALO, tf), F32),
                        pltpu.VMEM((tm, d), F32)],
        compiler_params=_params("arbitrary", "arbitrary"),
        name="conv_ffn",
    )(h2, h2, h2, x1, mod, w_up, w_up, conv_w, conv_b, w_down, ln_g, ln_b)


def _pad_cols(w, width):
    return jnp.pad(w, [(0, 0)] * (w.ndim - 1) + [(0, width - w.shape[-1])])


def _trunk(x, mods, ln_in, layers, alpha):
    batch, seq, d = x.shape
    x = x.reshape(batch * seq, d)
    for l, p in enumerate(layers):
        mod = mods[l]
        d_attn = p["w_attn_proj"].shape[0]
        if l == 0:
            x, qkv, zr = _in_projection(x, mod, p["w_in"], p["b_in"], seq, d_attn, ln_in=ln_in)
        else:
            qkv, zr = _in_projection(x, mod, p["w_in"], p["b_in"], seq, d_attn)
        attn = _attention(qkv, p["bias"], batch, seq)
        cu = _conv_branch(zr, p["conv_w"], p["conv_b"], p["conv_ln_g"], p["conv_ln_b"], seq)
        x1, h2 = _mix(attn, cu, zr, x, mod, p["w_attn_proj"], p["w_conv_proj"], p["w_out"],
                      p["ln_mix_g"], p["ln_mix_b"], seq, alpha)
        x = _ffn(h2, x1, mod, p["w_up"], p["ffn_conv_w"], p["ffn_conv_b"], p["w_down"],
                 p["ln_ffn_g"], p["ln_ffn_b"], seq, alpha)
    return x.reshape(batch, seq, d)


def kernel(x_prompt, x_sample, c_prompt, c_sample, ln_in_g, ln_in_b, w_mod, b_mod, w_in, b_in, na_rpb, w_attn_proj, conv_w, conv_b, conv_ln_g, conv_ln_b, w_conv_proj, w_out, ln_mix_g, ln_mix_b, w_up, ffn_conv_w, ffn_conv_b, w_down, ln_ffn_g, ln_ffn_b):
    depth, d, _ = w_mod.shape
    alpha = (2 * depth) ** 0.25
    d_ff = w_down.shape[1]
    fp = -(-d_ff // 512) * 512 if d_ff > 512 else -(-d_ff // LANES) * LANES

    nb_p, nb_s = c_prompt.shape[0], c_sample.shape[0]
    c_all = jnp.concatenate([c_prompt, c_sample], axis=0)
    mod_all = _modulation(c_all, w_mod, b_mod[:, None, :]).reshape(depth, nb_p + nb_s, N_MOD, d)
    mods_p = [mod_all[l, :nb_p] for l in range(depth)]
    mods_s = [mod_all[l, nb_p:] for l in range(depth)]

    row = lambda v: v[None, :]
    layers = []
    for l in range(depth):
        up_a, up_b = w_up[l, :, :d_ff], w_up[l, :, d_ff:]
        layers.append(dict(
            w_in=w_in[l].astype(BF16), b_in=row(b_in[l]),
            bias=_bias_table(na_rpb[l]),
            w_attn_proj=w_attn_proj[l].astype(BF16),
            conv_w=conv_w[l], conv_b=row(conv_b[l]),
            conv_ln_g=row(conv_ln_g[l]), conv_ln_b=row(conv_ln_b[l]),
            w_conv_proj=w_conv_proj[l].astype(BF16),
            w_out=w_out[l].astype(BF16),
            ln_mix_g=row(ln_mix_g[l]), ln_mix_b=row(ln_mix_b[l]),
            w_up=jnp.concatenate([_pad_cols(up_a, fp), _pad_cols(up_b, fp)], axis=1).astype(BF16),
            ffn_conv_w=_pad_cols(ffn_conv_w[l], fp), ffn_conv_b=_pad_cols(row(ffn_conv_b[l]), fp),
            w_down=jnp.pad(w_down[l], ((0, fp - d_ff), (0, 0))).astype(BF16),
            ln_ffn_g=row(ln_ffn_g[l]), ln_ffn_b=row(ln_ffn_b[l]),
        ))
    ln_in = (row(ln_in_g), row(ln_in_b))
    y_prompt = _trunk(x_prompt, mods_p, ln_in, layers, alpha)
    y_sample = _trunk(x_sample, mods_s, ln_in, layers, alpha)
    return (y_prompt, y_sample)
```

```python
import functools
import math

import jax
import jax.numpy as jnp
from jax import lax
from jax.experimental import pallas as pl
from jax.experimental.pallas import tpu as pltpu

GRID_W = 64
NA_HEADS = 16
NA_HEAD_DIM = 64
NA_KH_MAX = 8
NA_KW = 16
CONV_K = 31
FFN_CONV_K = 3
N_MOD = 6
LN_EPS = 1e-5
NEG_INF = -1e9

LANES = 128
SUBLANES = 8
HALO = 16
CONV_CHUNK = 32
VMEM_LIMIT = 56 * 1024 * 1024

F32 = jnp.float32
BF16 = jnp.bfloat16


def _tile(n, pref, mult):
    t = min(pref, n)
    t -= t % mult
    while t > mult and n % t:
        t -= mult
    assert t >= mult and n % t == 0, (n, pref, mult)
    return t


def _ln(x, g, b):
    mu = jnp.mean(x, axis=-1, keepdims=True)
    xc = x - mu
    var = jnp.mean(xc * xc, axis=-1, keepdims=True)
    return xc * lax.rsqrt(var + LN_EPS) * g + b


def _params(*sem):
    return pltpu.CompilerParams(dimension_semantics=sem, vmem_limit_bytes=VMEM_LIMIT)


def _resident(shape):
    return pl.BlockSpec(shape, lambda *_: (0,) * len(shape), pipeline_mode=pl.Buffered(1))


def _mod_kernel(c_ref, w_ref, b_ref, o_ref):
    c = c_ref[...]
    a = c * jax.nn.sigmoid(c)
    o_ref[...] = jnp.dot(a, w_ref[...], preferred_element_type=F32) + b_ref[...]


def _modulation(c_all, w_mod, b_mod):
    n_layers, d, e = w_mod.shape
    r = c_all.shape[0]
    tn = _tile(e, 1024, LANES)
    return pl.pallas_call(
        _mod_kernel,
        grid=(n_layers, e // tn),
        in_specs=[
            pl.BlockSpec((r, d), lambda l, j: (0, 0)),
            pl.BlockSpec((None, d, tn), lambda l, j: (l, 0, j)),
            pl.BlockSpec((None, 1, tn), lambda l, j: (l, 0, j)),
        ],
        out_specs=pl.BlockSpec((None, r, tn), lambda l, j: (l, 0, j)),
        out_shape=jax.ShapeDtypeStruct((n_layers, r, e), F32),
        compiler_params=_params("arbitrary", "arbitrary"),
        name="modulation",
    )(c_all, w_mod, b_mod)


def _inproj_kernel(*refs, first_layer, n_qkv_tiles):
    if first_layer:
        x_ref, mod_ref, g_ref, b_ref, w_ref, bias_ref, x0_ref, qkv_ref, zr_ref, h_scr = refs
    else:
        x_ref, mod_ref, w_ref, bias_ref, qkv_ref, zr_ref, h_scr = refs
    j = pl.program_id(1)

    @pl.when(j == 0)
    def _():
        x = x_ref[...]
        if first_layer:
            x = _ln(x, g_ref[...], b_ref[...])
            x0_ref[...] = x
        h_scr[...] = (x * (1.0 + mod_ref[1:2, :]) + mod_ref[0:1, :]).astype(BF16)

    res = jnp.dot(h_scr[...], w_ref[...], preferred_element_type=F32) + bias_ref[...]

    @pl.when(j < n_qkv_tiles)
    def _():
        for p in range(qkv_ref.shape[0]):
            qkv_ref[p] = res[:, p * LANES:(p + 1) * LANES].astype(BF16)

    @pl.when(j >= n_qkv_tiles)
    def _():
        zr_ref[...] = res.astype(BF16)


def _in_projection(x, mod, w_in, b_in, seq, d_attn, ln_in=None):
    n, d = x.shape
    e = w_in.shape[1]
    first = ln_in is not None
    tm = _tile(seq, 512 if first else 1024, HALO)
    tn = _tile(math.gcd(3 * d_attn, e - 3 * d_attn), 1024, LANES)
    nq = 3 * d_attn // tn
    row = lambda i, j: (i, 0)
    in_specs = [pl.BlockSpec((tm, d), row),
                pl.BlockSpec((None, N_MOD, d), lambda i, j: (i * tm // seq, 0, 0))]
    args = [x, mod]
    if first:
        in_specs += [pl.BlockSpec((1, d), lambda i, j: (0, 0))] * 2
        args += list(ln_in)
    in_specs += [pl.BlockSpec((d, tn), lambda i, j: (0, j)),
                 pl.BlockSpec((1, tn), lambda i, j: (0, j))]
    args += [w_in, b_in]
    out_specs = [pl.BlockSpec((tn // LANES, tm, LANES), lambda i, j: (jnp.minimum(j, nq - 1), i, 0)),
                 pl.BlockSpec((tm, tn), lambda i, j: (i, jnp.maximum(j - nq, 0)))]
    out_shape = [jax.ShapeDtypeStruct((3 * d_attn // LANES, n, LANES), BF16),
                 jax.ShapeDtypeStruct((n, e - 3 * d_attn), BF16)]
    if first:
        out_specs = [pl.BlockSpec((tm, d), row)] + out_specs
        out_shape = [jax.ShapeDtypeStruct((n, d), F32)] + out_shape
    return pl.pallas_call(
        functools.partial(_inproj_kernel, first_layer=first, n_qkv_tiles=nq),
        grid=(n // tm, e // tn),
        in_specs=in_specs,
        out_specs=out_specs,
        out_shape=out_shape,
        scratch_shapes=[pltpu.VMEM((tm, d), BF16)],
        compiler_params=_params("arbitrary", "arbitrary"),
        name="in_projection",
    )(*args)


def _attn_kernel(q_ref, k_ref, v_ref, bias_ref, o_ref, *, rows, rows_per_step):
    kh = NA_KH_MAX
    nk = kh * GRID_W
    step = pl.program_id(2)
    low_half = lax.broadcasted_iota(jnp.int32, (GRID_W, LANES), 1) < NA_HEAD_DIM
    kc = lax.broadcasted_iota(jnp.int32, (nk, LANES), 0) & (GRID_W - 1)
    qc = lax.broadcasted_iota(jnp.int32, (nk, LANES), 1) & (GRID_W - 1)
    c0 = jnp.clip(qc - NA_KW // 2, 0, GRID_W - NA_KW)
    mask = (kc >= c0) & (kc < c0 + NA_KW)
    scale = NA_HEAD_DIM ** -0.5

    for rr in range(rows_per_step):
        r = step * rows_per_step + rr
        r0 = jnp.clip(r - kh // 2, 0, rows - kh)
        shift = r0 - r + (NA_KH_MAX - 1)
        koff = pl.multiple_of(r0 * GRID_W, GRID_W)
        qoff = rr * GRID_W
        k_win = k_ref[pl.ds(koff, nk), :]
        v_win = v_ref[pl.ds(koff, nk), :]
        q = q_ref[pl.ds(qoff, GRID_W), :] * scale
        zero = jnp.zeros_like(q)
        q_blocks = jnp.concatenate([jnp.where(low_half, q, zero), jnp.where(low_half, zero, q)], axis=0)
        s = lax.dot_general(k_win, q_blocks, (((1,), (1,)), ((), ())), preferred_element_type=F32)
        s = jnp.where(mask, s + bias_ref[shift], NEG_INF)
        m = jnp.max(s, axis=0, keepdims=True)
        p = jnp.exp(s - m)
        inv = 1.0 / jnp.sum(p, axis=0, keepdims=True)
        pv_t = lax.dot_general(v_win, p.astype(BF16), (((0,), (0,)), ((), ())), preferred_element_type=F32)
        pv = (pv_t * inv).T
        o_ref[pl.ds(qoff, GRID_W), :] = jnp.where(low_half, pv[:GRID_W], pv[GRID_W:]).astype(o_ref.dtype)


def _bias_table(rpb):
    heads = rpb.shape[0]
    cols = jnp.arange(GRID_W)
    dc = jnp.clip(cols[None, :] - cols[:, None] + NA_KW - 1, 0, 2 * NA_KW - 2)
    by_col = rpb[:, :, dc]
    dr = jnp.arange(NA_KH_MAX)[:, None] + jnp.arange(NA_KH_MAX)[None, :]
    t = by_col[:, dr]
    t = t.reshape(heads // 2, 2, NA_KH_MAX, NA_KH_MAX, GRID_W, GRID_W).transpose(0, 2, 3, 5, 1, 4)
    return t.reshape(heads // 2, NA_KH_MAX, NA_KH_MAX * GRID_W, 2 * GRID_W)


def _attention(qkv, bias, batch, seq):
    planes, n, _ = qkv.shape
    pairs = planes // 3
    rows = seq // GRID_W
    assert seq % GRID_W == 0 and rows >= NA_KH_MAX
    rps = _tile(rows, 8, 1)
    tq = rps * GRID_W
    steps = rows // rps
    return pl.pallas_call(
        functools.partial(_attn_kernel, rows=rows, rows_per_step=rps),
        grid=(batch, pairs, steps),
        in_specs=[
            pl.BlockSpec((None, tq, LANES), lambda b, p, s: (p, b * steps + s, 0)),
            pl.BlockSpec((None, seq, LANES), lambda b, p, s: (pairs + p, b, 0)),
            pl.BlockSpec((None, seq, LANES), lambda b, p, s: (2 * pairs + p, b, 0)),
            pl.BlockSpec((None,) + bias.shape[1:], lambda b, p, s: (p, 0, 0, 0)),
        ],
        out_specs=pl.BlockSpec((tq, LANES), lambda b, p, s: (b * steps + s, p)),
        out_shape=jax.ShapeDtypeStruct((n, pairs * LANES), BF16),
        compiler_params=_params("arbitrary", "arbitrary", "arbitrary"),
        name="neighbourhood_attention",
    )(qkv, qkv, qkv, bias)


def _conv_kernel(uv_ref, ug_ref, pv_ref, pg_ref, nv_ref, ng_ref, w_ref, cb_ref, g_ref, b_ref, o_ref, sh, conv,
                 *, tiles_per_seq):
    tt = uv_ref.shape[0]
    t = pl.program_id(0) % tiles_per_seq

    def glu(v_ref, gate_ref):
        return v_ref[...].astype(F32) * jax.nn.sigmoid(gate_ref[...].astype(F32))

    sh[0, 0:HALO, :] = jnp.where(t == 0, 0.0, glu(pv_ref, pg_ref))
    sh[0, HALO:HALO + tt, :] = glu(uv_ref, ug_ref)
    sh[0, HALO + tt:, :] = jnp.where(t == tiles_per_seq - 1, 0.0, glu(nv_ref, ng_ref))
    first_tap = HALO - CONV_K // 2
    span = tt + (first_tap + CONV_K - 1) // SUBLANES * SUBLANES
    for s in range(1, SUBLANES):
        sh[s, 0:span, :] = sh[0, s:s + span, :]

    def chunk(ci, carry):
        base = ci * CONV_CHUNK
        acc = jnp.zeros((CONV_CHUNK, conv.shape[1]), F32) + cb_ref[...]
        for k in range(CONV_K):
            tiles, phase = divmod(first_tap + k, SUBLANES)
            start = pl.multiple_of(base + tiles * SUBLANES, SUBLANES)
            acc = acc + w_ref[k:k + 1, :] * sh[phase, pl.ds(start, CONV_CHUNK), :]
        conv[pl.ds(pl.multiple_of(base, CONV_CHUNK), CONV_CHUNK), :] = acc
        return carry

    lax.fori_loop(0, tt // CONV_CHUNK, chunk, 0)
    y = _ln(conv[...], g_ref[...], b_ref[...])
    o_ref[...] = (y * jax.nn.sigmoid(y)).astype(o_ref.dtype)


def _conv_branch(zr, conv_w, conv_b, ln_g, ln_b, seq):
    n = zr.shape[0]
    c = conv_w.shape[1]
    tt = _tile(seq, 512, HALO)
    th = tt // HALO
    last_halo = n // HALO - 1
    prev = lambda col: (lambda i: (jnp.maximum(i * th - 1, 0), col))
    nxt = lambda col: (lambda i: (jnp.minimum((i + 1) * th, last_halo), col))
    vec = pl.BlockSpec((1, c), lambda i: (0, 0))
    return pl.pallas_call(
        functools.partial(_conv_kernel, tiles_per_seq=seq // tt),
        grid=(n // tt,),
        in_specs=[
            pl.BlockSpec((tt, c), lambda i: (i, 0)),
            pl.BlockSpec((tt, c), lambda i: (i, 1)),
            pl.BlockSpec((HALO, c), prev(0)),
            pl.BlockSpec((HALO, c), prev(1)),
            pl.BlockSpec((HALO, c), nxt(0)),
            pl.BlockSpec((HALO, c), nxt(1)),
            pl.BlockSpec((CONV_K, c), lambda i: (0, 0)),
            vec, vec, vec,
        ],
        out_specs=pl.BlockSpec((tt, c), lambda i: (i, 0)),
        out_shape=jax.ShapeDtypeStruct((n, c), BF16),
        scratch_shapes=[pltpu.VMEM((SUBLANES, tt + 2 * HALO, c), F32), pltpu.VMEM((tt, c), F32)],
        compiler_params=_params("arbitrary"),
        name="conv_branch",
    )(zr, zr, zr, zr, zr, zr, conv_w, conv_b, ln_g, ln_b)


def _mix_kernel(attn_ref, cu_ref, ga_ref, gc_ref, x_ref, mod_ref, wa_ref, wc_ref, wo_ref, g_ref, b_ref,
                x1_ref, h2_ref, *, alpha):
    y_a = jnp.dot(attn_ref[...], wa_ref[...], preferred_element_type=F32)
    y_c = jnp.dot(cu_ref[...], wc_ref[...], preferred_element_type=F32)
    y = jax.nn.sigmoid(ga_ref[...].astype(F32)) * y_a + jax.nn.sigmoid(gc_ref[...].astype(F32)) * y_c
    y = jnp.dot(y.astype(BF16), wo_ref[...], preferred_element_type=F32)
    x1 = _ln(alpha * x_ref[...] + mod_ref[2:3, :] * y, g_ref[...], b_ref[...])
    x1_ref[...] = x1
    h2_ref[...] = (x1 * (1.0 + mod_ref[4:5, :]) + mod_ref[3:4, :]).astype(BF16)


def _mix(attn, cu, zr, x, mod, w_attn_proj, w_conv_proj, w_out, ln_g, ln_b, seq, alpha):
    n, d = x.shape
    c = cu.shape[1]
    da = attn.shape[1]
    assert (2 * c) % d == 0
    gate_blk = 2 * c // d
    tm = _tile(seq, 256, HALO)
    row = lambda i: (i, 0)
    return pl.pallas_call(
        functools.partial(_mix_kernel, alpha=alpha),
        grid=(n // tm,),
        in_specs=[
            pl.BlockSpec((tm, da), row),
            pl.BlockSpec((tm, c), row),
            pl.BlockSpec((tm, d), lambda i: (i, gate_blk)),
            pl.BlockSpec((tm, d), lambda i: (i, gate_blk + 1)),
            pl.BlockSpec((tm, d), row),
            pl.BlockSpec((None, N_MOD, d), lambda i: (i * tm // seq, 0, 0)),
            _resident(w_attn_proj.shape),
            _resident(w_conv_proj.shape),
            _resident(w_out.shape),
            _resident(ln_g.shape),
            _resident(ln_b.shape),
        ],
        out_specs=[pl.BlockSpec((tm, d), row), pl.BlockSpec((tm, d), row)],
        out_shape=[jax.ShapeDtypeStruct((n, d), F32), jax.ShapeDtypeStruct((n, d), BF16)],
        compiler_params=_params("arbitrary"),
        name="merge_out_projection",
    )(attn, cu, zr, zr, x, mod, w_attn_proj, w_conv_proj, w_out, ln_g, ln_b)


def _ffn_kernel(h_ref, hp_ref, hn_ref, x_ref, mod_ref, wa_ref, wb_ref, cw_ref, cb_ref, wd_ref, g_ref, b_ref,
                o_ref, h_ext, a_ext, acc, *, tiles_per_seq, alpha):
    tm = h_ref.shape[0]
    j = pl.program_id(1)
    t = pl.program_id(0) % tiles_per_seq

    @pl.when(j == 0)
    def _():
        h_ext[0:HALO, :] = jnp.where(t == 0, jnp.zeros_like(hp_ref), hp_ref[...])
        h_ext[HALO:HALO + tm, :] = h_ref[...]
        h_ext[HALO + tm:, :] = jnp.where(t == tiles_per_seq - 1, jnp.zeros_like(hn_ref), hn_ref[...])
        acc[...] = jnp.zeros_like(acc)

    a_ext[...] = jnp.dot(h_ext[...], wa_ref[...], preferred_element_type=F32)
    gate = jnp.dot(h_ref[...], wb_ref[...], preferred_element_type=F32)
    a = cb_ref[...]
    for k in range(FFN_CONV_K):
        start = HALO - FFN_CONV_K // 2 + k
        a = a + cw_ref[k:k + 1, :] * a_ext[start:start + tm, :]
    act = 0.5 * a * (1.0 + lax.erf(a * (2.0 ** -0.5))) * gate
    acc[...] += jnp.dot(act.astype(BF16), wd_ref[...], preferred_element_type=F32)

    @pl.when(j == pl.num_programs(1) - 1)
    def _():
        o_ref[...] = _ln(alpha * x_ref[...] + mod_ref[5:6, :] * acc[...], g_ref[...], b_ref[...])


def _ffn(h2, x1, mod, w_up, conv_w, conv_b, w_down, ln_g, ln_b, seq, alpha):
    n, d = x1.shape
    fp = w_down.shape[0]
    tm = _tile(seq, 512, HALO)
    tf = _tile(fp, 512, LANES)
    nf = fp // tf
    th = tm // HALO
    last_halo = n // HALO - 1
    row = lambda i, j: (i, 0)
    vec = pl.BlockSpec((1, d), lambda i, j: (0, 0))
    return pl.pallas_call(
        functools.partial(_ffn_kernel, tiles_per_seq=seq // tm, alpha=alpha),
        grid=(n // tm, nf),
        in_specs=[
            pl.BlockSpec((tm, d), row),
            pl.BlockSpec((HALO, d), lambda i, j: (jnp.maximum(i * th - 1, 0), 0)),
            pl.BlockSpec((HALO, d), lambda i, j: (jnp.minimum((i + 1) * th, last_halo), 0)),
            pl.BlockSpec((tm, d), row),
            pl.BlockSpec((None, N_MOD, d), lambda i, j: (i * tm // seq, 0, 0)),
            pl.BlockSpec((d, tf), lambda i, j: (0, j)),
            pl.BlockSpec((d, tf), lambda i, j: (0, nf + j)),
            pl.BlockSpec((FFN_CONV_K, tf), lambda i, j: (0, j)),
            pl.BlockSpec((1, tf), lambda i, j: (0, j)),
            pl.BlockSpec((tf, d), lambda i, j: (j, 0)),
            vec, vec,
        ],
        out_specs=pl.BlockSpec((tm, d), row),
        out_shape=jax.ShapeDtypeStruct((n, d), F32),
        scratch_shapes=[pltpu.VMEM((tm + 2 * HALO, d), BF16),
                        pltpu.VMEM((tm + 2 * HALO, tf), F32),
                        pltpu.VMEM((tm, d), F32)],
        compiler_params=_params("arbitrary", "arbitrary"),
        name="conv_ffn",
    )(h2, h2, h2, x1, mod, w_up, w_up, conv_w, conv_b, w_down, ln_g, ln_b)


def _pad_cols(w, width):
    return jnp.pad(w, [(0, 0)] * (w.ndim - 1) + [(0, width - w.shape[-1])])


def _trunk(x, mods, ln_in, layers, alpha):
    batch, seq, d = x.shape
    x = x.reshape(batch * seq, d)
    for l, p in enumerate(layers):
        mod = mods[l]
        d_attn = p["w_attn_proj"].shape[0]
        if l == 0:
            x, qkv, zr = _in_projection(x, mod, p["w_in"], p["b_in"], seq, d_attn, ln_in=ln_in)
        else:
            qkv, zr = _in_projection(x, mod, p["w_in"], p["b_in"], seq, d_attn)
        attn = _attention(qkv, p["bias"], batch, seq)
        cu = _conv_branch(zr, p["conv_w"], p["conv_b"], p["conv_ln_g"], p["conv_ln_b"], seq)
        x1, h2 = _mix(attn, cu, zr, x, mod, p["w_attn_proj"], p["w_conv_proj"], p["w_out"],
                      p["ln_mix_g"], p["ln_mix_b"], seq, alpha)
        x = _ffn(h2, x1, mod, p["w_up"], p["ffn_conv_w"], p["ffn_conv_b"], p["w_down"],
                 p["ln_ffn_g"], p["ln_ffn_b"], seq, alpha)
    return x.reshape(batch, seq, d)


def kernel(x_prompt, x_sample, c_prompt, c_sample, ln_in_g, ln_in_b, w_mod, b_mod, w_in, b_in, na_rpb, w_attn_proj, conv_w, conv_b, conv_ln_g, conv_ln_b, w_conv_proj, w_out, ln_mix_g, ln_mix_b, w_up, ffn_conv_w, ffn_conv_b, w_down, ln_ffn_g, ln_ffn_b):
    depth, d, _ = w_mod.shape
    alpha = (2 * depth) ** 0.25
    d_ff = w_down.shape[1]
    fp = -(-d_ff // 512) * 512 if d_ff > 512 else -(-d_ff // LANES) * LANES

    nb_p, nb_s = c_prompt.shape[0], c_sample.shape[0]
    c_all = jnp.concatenate([c_prompt, c_sample], axis=0)
    mod_all = _modulation(c_all, w_mod, b_mod[:, None, :]).reshape(depth, nb_p + nb_s, N_MOD, d)
    mods_p = [mod_all[l, :nb_p] for l in range(depth)]
    mods_s = [mod_all[l, nb_p:] for l in range(depth)]

    row = lambda v: v[None, :]
    layers = []
    for l in range(depth):
        up_a, up_b = w_up[l, :, :d_ff], w_up[l, :, d_ff:]
        layers.append(dict(
            w_in=w_in[l].astype(BF16), b_in=row(b_in[l]),
            bias=_bias_table(na_rpb[l]),
            w_attn_proj=w_attn_proj[l].astype(BF16),
            conv_w=conv_w[l], conv_b=row(conv_b[l]),
            conv_ln_g=row(conv_ln_g[l]), conv_ln_b=row(conv_ln_b[l]),
            w_conv_proj=w_conv_proj[l].astype(BF16),
            w_out=w_out[l].astype(BF16),
            ln_mix_g=row(ln_mix_g[l]), ln_mix_b=row(ln_mix_b[l]),
            w_up=jnp.concatenate([_pad_cols(up_a, fp), _pad_cols(up_b, fp)], axis=1).astype(BF16),
            ffn_conv_w=_pad_cols(ffn_conv_w[l], fp), ffn_conv_b=_pad_cols(row(ffn_conv_b[l]), fp),
            w_down=jnp.pad(w_down[l], ((0, fp - d_ff), (0, 0))).astype(BF16),
            ln_ffn_g=row(ln_ffn_g[l]), ln_ffn_b=row(ln_ffn_b[l]),
        ))
    ln_in = (row(ln_in_g), row(ln_in_b))
    y_prompt = _trunk(x_prompt, mods_p, ln_in, layers, alpha)
    y_sample = _trunk(x_sample, mods_s, ln_in, layers, alpha)
    return (y_prompt, y_sample)
```

```python
import functools
import math

import jax
import jax.numpy as jnp
from jax import lax
from jax.experimental import pallas as pl
from jax.experimental.pallas import tpu as pltpu

GRID_W = 64
NA_HEADS = 16
NA_HEAD_DIM = 64
NA_KH_MAX = 8
NA_KW = 16
CONV_K = 31
FFN_CONV_K = 3
N_MOD = 6
LN_EPS = 1e-5
NEG_INF = -1e9

LANES = 128
SUBLANES = 8
HALO = 16
CONV_CHUNK = 32
VMEM_LIMIT = 56 * 1024 * 1024

F32 = jnp.float32
BF16 = jnp.bfloat16


def _tile(n, pref, mult):
    t = min(pref, n)
    t -= t % mult
    while t > mult and n % t:
        t -= mult
    assert t >= mult and n % t == 0, (n, pref, mult)
    return t


def _ln(x, g, b):
    mu = jnp.mean(x, axis=-1, keepdims=True)
    xc = x - mu
    var = jnp.mean(xc * xc, axis=-1, keepdims=True)
    return xc * lax.rsqrt(var + LN_EPS) * g + b


def _params(*sem):
    return pltpu.CompilerParams(dimension_semantics=sem, vmem_limit_bytes=VMEM_LIMIT)


def _resident(shape):
    return pl.BlockSpec(shape, lambda *_: (0,) * len(shape), pipeline_mode=pl.Buffered(1))


def _mod_kernel(c_ref, w_ref, b_ref, o_ref):
    c = c_ref[...]
    a = c * jax.nn.sigmoid(c)
    o_ref[...] = jnp.dot(a, w_ref[...], preferred_element_type=F32) + b_ref[...]


def _modulation(c_all, w_mod, b_mod):
    n_layers, d, e = w_mod.shape
    r = c_all.shape[0]
    tn = _tile(e, 1024, LANES)
    return pl.pallas_call(
        _mod_kernel,
        grid=(n_layers, e // tn),
        in_specs=[
            pl.BlockSpec((r, d), lambda l, j: (0, 0)),
            pl.BlockSpec((None, d, tn), lambda l, j: (l, 0, j)),
            pl.BlockSpec((None, 1, tn), lambda l, j: (l, 0, j)),
        ],
        out_specs=pl.BlockSpec((None, r, tn), lambda l, j: (l, 0, j)),
        out_shape=jax.ShapeDtypeStruct((n_layers, r, e), F32),
        compiler_params=_params("arbitrary", "arbitrary"),
        name="modulation",
    )(c_all, w_mod, b_mod)


def _glu_kernel(*refs, first_layer):
    if first_layer:
        x_ref, mod_ref, g_ref, b_ref, wv_ref, wg_ref, bv_ref, bg_ref, x0_ref, h_ref, u_ref = refs
    else:
        x_ref, mod_ref, wv_ref, wg_ref, bv_ref, bg_ref, h_ref, u_ref = refs
    x = x_ref[...]
    if first_layer:
        x = _ln(x, g_ref[...], b_ref[...])
        x0_ref[...] = x
    h = (x * (1.0 + mod_ref[1:2, :]) + mod_ref[0:1, :]).astype(BF16)
    h_ref[...] = h
    value = jnp.dot(h, wv_ref[...], preferred_element_type=F32) + bv_ref[...]
    gate = jnp.dot(h, wg_ref[...], preferred_element_type=F32) + bg_ref[...]
    u_ref[...] = (value * jax.nn.sigmoid(gate)).astype(BF16)


def _glu_projection(x, mod, w_in, b_in, seq, d_attn, c, ln_in=None):
    n, d = x.shape
    assert (3 * d_attn) % c == 0
    value_blk = 3 * d_attn // c
    first = ln_in is not None
    tm = _tile(seq, 512, HALO)
    row = lambda i: (i, 0)
    in_specs = [pl.BlockSpec((tm, d), row),
                pl.BlockSpec((None, N_MOD, d), lambda i: (i * tm // seq, 0, 0))]
    args = [x, mod]
    if first:
        in_specs += [_resident(ln_in[0].shape)] * 2
        args += list(ln_in)
    single = dict(pipeline_mode=pl.Buffered(1))
    in_specs += [pl.BlockSpec((d, c), lambda i: (0, value_blk), **single),
                 pl.BlockSpec((d, c), lambda i: (0, value_blk + 1), **single),
                 pl.BlockSpec((1, c), lambda i: (0, value_blk), **single),
                 pl.BlockSpec((1, c), lambda i: (0, value_blk + 1), **single)]
    args += [w_in, w_in, b_in, b_in]
    out_specs = [pl.BlockSpec((tm, d), row), pl.BlockSpec((tm, c), row)]
    out_shape = [jax.ShapeDtypeStruct((n, d), BF16), jax.ShapeDtypeStruct((n, c), BF16)]
    if first:
        out_specs = [pl.BlockSpec((tm, d), row)] + out_specs
        out_shape = [jax.ShapeDtypeStruct((n, d), F32)] + out_shape
    return pl.pallas_call(
        functools.partial(_glu_kernel, first_layer=first),
        grid=(n // tm,),
        in_specs=in_specs,
        out_specs=out_specs,
        out_shape=out_shape,
        compiler_params=_params("arbitrary"),
        name="glu_projection",
    )(*args)


def _inproj_kernel(h_ref, w_ref, bias_ref, qkv_ref, gates_ref, *, n_qkv_tiles):
    j = pl.program_id(1)
    res = (jnp.dot(h_ref[...], w_ref[...], preferred_element_type=F32) + bias_ref[...]).astype(BF16)

    @pl.when(j < n_qkv_tiles)
    def _():
        for p in range(qkv_ref.shape[0]):
            qkv_ref[p] = res[:, p * LANES:(p + 1) * LANES]

    @pl.when(j >= n_qkv_tiles)
    def _():
        gates_ref[...] = res


def _in_projection(h, w_in, b_in, seq, d_attn, c):
    n, d = h.shape
    e = w_in.shape[1]
    n_gate = e - 3 * d_attn - 2 * c
    tm = _tile(seq, 1024, HALO)
    tn = _tile(math.gcd(math.gcd(3 * d_attn, 2 * c), n_gate), 1024, LANES)
    nq = 3 * d_attn // tn
    skip = 2 * c // tn
    col = lambda i, j: (0, jnp.where(j < nq, j, j + skip))
    return pl.pallas_call(
        functools.partial(_inproj_kernel, n_qkv_tiles=nq),
        grid=(n // tm, nq + n_gate // tn),
        in_specs=[
            pl.BlockSpec((tm, d), lambda i, j: (i, 0)),
            pl.BlockSpec((d, tn), col),
            pl.BlockSpec((1, tn), col),
        ],
        out_specs=[
            pl.BlockSpec((tn // LANES, tm, LANES), lambda i, j: (jnp.minimum(j, nq - 1), i, 0)),
            pl.BlockSpec((tm, tn), lambda i, j: (i, jnp.maximum(j - nq, 0))),
        ],
        out_shape=[jax.ShapeDtypeStruct((3 * d_attn // LANES, n, LANES), BF16),
                   jax.ShapeDtypeStruct((n, n_gate), BF16)],
        compiler_params=_params("arbitrary", "arbitrary"),
        name="in_projection",
    )(h, w_in, b_in)


def _conv_kernel(u_ref, up_ref, un_ref, w_ref, cb_ref, g_ref, b_ref, o_ref, sh, conv, *, tiles_per_seq):
    tt = u_ref.shape[0]
    t = pl.program_id(0) % tiles_per_seq
    sh[0, 0:HALO, :] = jnp.where(t == 0, 0.0, up_ref[...].astype(F32))
    sh[0, HALO:HALO + tt, :] = u_ref[...].astype(F32)
    sh[0, HALO + tt:, :] = jnp.where(t == tiles_per_seq - 1, 0.0, un_ref[...].astype(F32))
    first_tap = HALO - CONV_K // 2
    span = tt + (first_tap + CONV_K - 1) // SUBLANES * SUBLANES
    for s in range(1, SUBLANES):
        sh[s, 0:span, :] = sh[0, s:s + span, :]

    def chunk(ci, carry):
        base = ci * CONV_CHUNK
        accs = [cb_ref[...]] * (CONV_CHUNK // SUBLANES)
        for k in range(CONV_K):
            tiles, phase = divmod(first_tap + k, SUBLANES)
            w_k = w_ref[k]
            for i in range(len(accs)):
                start = pl.multiple_of(base + (tiles + i) * SUBLANES, SUBLANES)
                accs[i] = accs[i] + w_k * sh[phase, pl.ds(start, SUBLANES), :]
        for i, acc in enumerate(accs):
            conv[pl.ds(pl.multiple_of(base + i * SUBLANES, SUBLANES), SUBLANES), :] = acc
        return carry

    lax.fori_loop(0, tt // CONV_CHUNK, chunk, 0)
    y = _ln(conv[...], g_ref[...], b_ref[...])
    o_ref[...] = (y * jax.nn.sigmoid(y)).astype(o_ref.dtype)


def _conv_branch(u, conv_w, conv_b, ln_g, ln_b, seq):
    n, c = u.shape
    tt = _tile(seq, 512, CONV_CHUNK)
    th = tt // HALO
    last_halo = n // HALO - 1
    vec = pl.BlockSpec((1, c), lambda i: (0, 0))
    return pl.pallas_call(
        functools.partial(_conv_kernel, tiles_per_seq=seq // tt),
        grid=(n // tt,),
        in_specs=[
            pl.BlockSpec((tt, c), lambda i: (i, 0)),
            pl.BlockSpec((HALO, c), lambda i: (jnp.maximum(i * th - 1, 0), 0)),
            pl.BlockSpec((HALO, c), lambda i: (jnp.minimum((i + 1) * th, last_halo), 0)),
            pl.BlockSpec((CONV_K, SUBLANES, c), lambda i: (0, 0, 0)),
            pl.BlockSpec((SUBLANES, c), lambda i: (0, 0)),
            vec, vec,
        ],
        out_specs=pl.BlockSpec((tt, c), lambda i: (i, 0)),
        out_shape=jax.ShapeDtypeStruct((n, c), BF16),
        scratch_shapes=[pltpu.VMEM((SUBLANES, tt + 2 * HALO, c), F32), pltpu.VMEM((tt, c), F32)],
        compiler_params=_params("arbitrary"),
        name="conv_branch",
    )(u, u, u, conv_w, conv_b, ln_g, ln_b)


def _attn_kernel(q_ref, k_ref, v_ref, bias_ref, o_ref, *, rows, rows_per_step):
    kh = NA_KH_MAX
    nk = kh * GRID_W
    step = pl.program_id(2)
    low_half = lax.broadcasted_iota(jnp.int32, (GRID_W, LANES), 1) < NA_HEAD_DIM
    kc = lax.broadcasted_iota(jnp.int32, (nk, LANES), 0) & (GRID_W - 1)
    qc = lax.broadcasted_iota(jnp.int32, (nk, LANES), 1) & (GRID_W - 1)
    c0 = jnp.clip(qc - NA_KW // 2, 0, GRID_W - NA_KW)
    mask = (kc >= c0) & (kc < c0 + NA_KW)
    scale = NA_HEAD_DIM ** -0.5

    for rr in range(rows_per_step):
        r = step * rows_per_step + rr
        r0 = jnp.clip(r - kh // 2, 0, rows - kh)
        shift = r0 - r + (NA_KH_MAX - 1)
        koff = pl.multiple_of(r0 * GRID_W, GRID_W)
        qoff = rr * GRID_W
        k_win = k_ref[pl.ds(koff, nk), :]
        v_win = v_ref[pl.ds(koff, nk), :]
        q = q_ref[pl.ds(qoff, GRID_W), :] * scale
        zero = jnp.zeros_like(q)
        q_blocks = jnp.concatenate([jnp.where(low_half, q, zero), jnp.where(low_half, zero, q)], axis=0)
        s = lax.dot_general(k_win, q_blocks, (((1,), (1,)), ((), ())), preferred_element_type=F32)
        s = jnp.where(mask, s + bias_ref[shift], NEG_INF)
        m = jnp.max(s, axis=0, keepdims=True)
        p = jnp.exp(s - m)
        inv = 1.0 / jnp.sum(p, axis=0, keepdims=True)
        pv_t = lax.dot_general(v_win, p.astype(BF16), (((0,), (0,)), ((), ())), preferred_element_type=F32)
        pv = (pv_t * inv).T
        o_ref[pl.ds(qoff, GRID_W), :] = jnp.where(low_half, pv[:GRID_W], pv[GRID_W:]).astype(o_ref.dtype)


def _bias_table(rpb):
    heads = rpb.shape[0]
    cols = jnp.arange(GRID_W)
    dc = jnp.clip(cols[None, :] - cols[:, None] + NA_KW - 1, 0, 2 * NA_KW - 2)
    by_col = rpb[:, :, dc]
    dr = jnp.arange(NA_KH_MAX)[:, None] + jnp.arange(NA_KH_MAX)[None, :]
    t = by_col[:, dr]
    t = t.reshape(heads // 2, 2, NA_KH_MAX, NA_KH_MAX, GRID_W, GRID_W).transpose(0, 2, 3, 5, 1, 4)
    return t.reshape(heads // 2, NA_KH_MAX, NA_KH_MAX * GRID_W, 2 * GRID_W)


def _attention(qkv, bias, batch, seq):
    planes, n, _ = qkv.shape
    pairs = planes // 3
    rows = seq // GRID_W
    assert seq % GRID_W == 0 and rows >= NA_KH_MAX
    rps = _tile(rows, 16, 1)
    tq = rps * GRID_W
    steps = rows // rps
    return pl.pallas_call(
        functools.partial(_attn_kernel, rows=rows, rows_per_step=rps),
        grid=(batch, pairs, steps),
        in_specs=[
            pl.BlockSpec((None, tq, LANES), lambda b, p, s: (p, b * steps + s, 0)),
            pl.BlockSpec((None, seq, LANES), lambda b, p, s: (pairs + p, b, 0)),
            pl.BlockSpec((None, seq, LANES), lambda b, p, s: (2 * pairs + p, b, 0)),
            pl.BlockSpec((None,) + bias.shape[1:], lambda b, p, s: (p, 0, 0, 0)),
        ],
        out_specs=pl.BlockSpec((tq, LANES), lambda b, p, s: (b * steps + s, p)),
        out_shape=jax.ShapeDtypeStruct((n, pairs * LANES), BF16),
        compiler_params=_params("arbitrary", "arbitrary", "arbitrary"),
        name="neighbourhood_attention",
    )(qkv, qkv, qkv, bias)


def _mix_kernel(attn_ref, cu_ref, ga_ref, gc_ref, x_ref, mod_ref, wa_ref, wc_ref, wo_ref, g_ref, b_ref,
                x1_ref, h2_ref, *, alpha):
    y_a = jnp.dot(attn_ref[...], wa_ref[...], preferred_element_type=F32)
    y_c = jnp.dot(cu_ref[...], wc_ref[...], preferred_element_type=F32)
    y = jax.nn.sigmoid(ga_ref[...].astype(F32)) * y_a + jax.nn.sigmoid(gc_ref[...].astype(F32)) * y_c
    y = jnp.dot(y.astype(BF16), wo_ref[...], preferred_element_type=F32)
    x1 = _ln(alpha * x_ref[...] + mod_ref[2:3, :] * y, g_ref[...], b_ref[...])
    x1_ref[...] = x1
    h2_ref[...] = (x1 * (1.0 + mod_ref[4:5, :]) + mod_ref[3:4, :]).astype(BF16)


def _mix(attn, cu, gates, x, mod, w_attn_proj, w_conv_proj, w_out, ln_g, ln_b, seq, alpha):
    n, d = x.shape
    c = cu.shape[1]
    da = attn.shape[1]
    tm = _tile(seq, 256, HALO)
    row = lambda i: (i, 0)
    return pl.pallas_call(
        functools.partial(_mix_kernel, alpha=alpha),
        grid=(n // tm,),
        in_specs=[
            pl.BlockSpec((tm, da), row),
            pl.BlockSpec((tm, c), row),
            pl.BlockSpec((tm, d), lambda i: (i, 0)),
            pl.BlockSpec((tm, d), lambda i: (i, 1)),
            pl.BlockSpec((tm, d), row),
            pl.BlockSpec((None, N_MOD, d), lambda i: (i * tm // seq, 0, 0)),
            _resident(w_attn_proj.shape),
            _resident(w_conv_proj.shape),
            _resident(w_out.shape),
            _resident(ln_g.shape),
            _resident(ln_b.shape),
        ],
        out_specs=[pl.BlockSpec((tm, d), row), pl.BlockSpec((tm, d), row)],
        out_shape=[jax.ShapeDtypeStruct((n, d), F32), jax.ShapeDtypeStruct((n, d), BF16)],
        compiler_params=_params("arbitrary"),
        name="merge_out_projection",
    )(attn, cu, gates, gates, x, mod, w_attn_proj, w_conv_proj, w_out, ln_g, ln_b)


def _ffn_kernel(h_ref, hp_ref, hn_ref, x_ref, mod_ref, wa_ref, wb_ref, cw_ref, cb_ref, wd_ref, g_ref, b_ref,
                o_ref, h_ext, a_ext, acc, *, tiles_per_seq, alpha):
    tm = h_ref.shape[0]
    j = pl.program_id(1)
    t = pl.program_id(0) % tiles_per_seq

    @pl.when(j == 0)
    def _():
        h_ext[0:HALO, :] = jnp.where(t == 0, jnp.zeros_like(hp_ref), hp_ref[...])
        h_ext[HALO:HALO + tm, :] = h_ref[...]
        h_ext[HALO + tm:, :] = jnp.where(t == tiles_per_seq - 1, jnp.zeros_like(hn_ref), hn_ref[...])
        acc[...] = jnp.zeros_like(acc)

    a_ext[...] = jnp.dot(h_ext[...], wa_ref[...], preferred_element_type=F32)
    gate = jnp.dot(h_ref[...], wb_ref[...], preferred_element_type=F32)
    a = cb_ref[...]
    for k in range(FFN_CONV_K):
        start = HALO - FFN_CONV_K // 2 + k
        a = a + cw_ref[k:k + 1, :] * a_ext[start:start + tm, :]
    act = 0.5 * a * (1.0 + lax.erf(a * (2.0 ** -0.5))) * gate
    acc[...] += jnp.dot(act.astype(BF16), wd_ref[...], preferred_element_type=F32)

    @pl.when(j == pl.num_programs(1) - 1)
    def _():
        o_ref[...] = _ln(alpha * x_ref[...] + mod_ref[5:6, :] * acc[...], g_ref[...], b_ref[...])


def _ffn(h2, x1, mod, w_up_a, w_up_b, conv_w, conv_b, w_down, ln_g, ln_b, seq, alpha):
    n, d = x1.shape
    fp = w_down.shape[0]
    tm = _tile(seq, 512, HALO)
    tf = _tile(fp, 512, LANES)
    nf = fp // tf
    th = tm // HALO
    last_halo = n // HALO - 1
    row = lambda i, j: (i, 0)
    vec = pl.BlockSpec((1, d), lambda i, j: (0, 0))
    return pl.pallas_call(
        functools.partial(_ffn_kernel, tiles_per_seq=seq // tm, alpha=alpha),
        grid=(n // tm, nf),
        in_specs=[
            pl.BlockSpec((tm, d), row),
            pl.BlockSpec((HALO, d), lambda i, j: (jnp.maximum(i * th - 1, 0), 0)),
            pl.BlockSpec((HALO, d), lambda i, j: (jnp.minimum((i + 1) * th, last_halo), 0)),
            pl.BlockSpec((tm, d), row),
            pl.BlockSpec((None, N_MOD, d), lambda i, j: (i * tm // seq, 0, 0)),
            pl.BlockSpec((d, tf), lambda i, j: (0, j)),
            pl.BlockSpec((d, tf), lambda i, j: (0, j)),
            pl.BlockSpec((FFN_CONV_K, tf), lambda i, j: (0, j)),
            pl.BlockSpec((1, tf), lambda i, j: (0, j)),
            pl.BlockSpec((tf, d), lambda i, j: (j, 0)),
            vec, vec,
        ],
        out_specs=pl.BlockSpec((tm, d), row),
        out_shape=jax.ShapeDtypeStruct((n, d), F32),
        scratch_shapes=[pltpu.VMEM((tm + 2 * HALO, d), BF16),
                        pltpu.VMEM((tm + 2 * HALO, tf), F32),
                        pltpu.VMEM((tm, d), F32)],
        compiler_params=_params("arbitrary", "arbitrary"),
        name="conv_ffn",
    )(h2, h2, h2, x1, mod, w_up_a, w_up_b, conv_w, conv_b, w_down, ln_g, ln_b)


def _pad_cols(w, width):
    return jnp.pad(w, [(0, 0)] * (w.ndim - 1) + [(0, width - w.shape[-1])])


def _trunk(x, mods, ln_in, layers, alpha):
    batch, seq, d = x.shape
    x = x.reshape(batch * seq, d)
    for l, p in enumerate(layers):
        mod = mods[l]
        d_attn, c = p["w_attn_proj"].shape[0], p["w_conv_proj"].shape[0]
        if l == 0:
            x, h, u = _glu_projection(x, mod, p["w_in"], p["b_in"], seq, d_attn, c, ln_in=ln_in)
        else:
            h, u = _glu_projection(x, mod, p["w_in"], p["b_in"], seq, d_attn, c)
        qkv, gates = _in_projection(h, p["w_in"], p["b_in"], seq, d_attn, c)
        attn = _attention(qkv, p["bias"], batch, seq)
        cu = _conv_branch(u, p["conv_w"], p["conv_b"], p["conv_ln_g"], p["conv_ln_b"], seq)
        x1, h2 = _mix(attn, cu, gates, x, mod, p["w_attn_proj"], p["w_conv_proj"], p["w_out"],
                      p["ln_mix_g"], p["ln_mix_b"], seq, alpha)
        x = _ffn(h2, x1, mod, p["w_up_a"], p["w_up_b"], p["ffn_conv_w"], p["ffn_conv_b"], p["w_down"],
                 p["ln_ffn_g"], p["ln_ffn_b"], seq, alpha)
    return x.reshape(batch, seq, d)


def kernel(x_prompt, x_sample, c_prompt, c_sample, ln_in_g, ln_in_b, w_mod, b_mod, w_in, b_in, na_rpb, w_attn_proj, conv_w, conv_b, conv_ln_g, conv_ln_b, w_conv_proj, w_out, ln_mix_g, ln_mix_b, w_up, ffn_conv_w, ffn_conv_b, w_down, ln_ffn_g, ln_ffn_b):
    depth, d, _ = w_mod.shape
    alpha = (2 * depth) ** 0.25
    d_ff = w_down.shape[1]
    fp = -(-d_ff // 512) * 512 if d_ff > 512 else -(-d_ff // LANES) * LANES

    nb_p, nb_s = c_prompt.shape[0], c_sample.shape[0]
    c_all = jnp.concatenate([c_prompt, c_sample], axis=0)
    mod_all = _modulation(c_all, w_mod, b_mod[:, None, :]).reshape(depth, nb_p + nb_s, N_MOD, d)
    mods_p = [mod_all[l, :nb_p] for l in range(depth)]
    mods_s = [mod_all[l, nb_p:] for l in range(depth)]

    row = lambda v: v[None, :]
    layers = []
    for l in range(depth):
        up_a, up_b = w_up[l, :, :d_ff], w_up[l, :, d_ff:]
        layers.append(dict(
            w_in=w_in[l].astype(BF16), b_in=row(b_in[l]),
            bias=_bias_table(na_rpb[l]),
            w_attn_proj=w_attn_proj[l].astype(BF16),
            conv_w=jnp.broadcast_to(conv_w[l][:, None, :], (conv_w.shape[1], SUBLANES, conv_w.shape[2])),
            conv_b=jnp.broadcast_to(conv_b[l][None, :], (SUBLANES, conv_b.shape[1])),
            conv_ln_g=row(conv_ln_g[l]), conv_ln_b=row(conv_ln_b[l]),
            w_conv_proj=w_conv_proj[l].astype(BF16),
            w_out=w_out[l].astype(BF16),
            ln_mix_g=row(ln_mix_g[l]), ln_mix_b=row(ln_mix_b[l]),
            w_up_a=_pad_cols(up_a.astype(BF16), fp), w_up_b=_pad_cols(up_b.astype(BF16), fp),
            ffn_conv_w=_pad_cols(ffn_conv_w[l], fp), ffn_conv_b=_pad_cols(row(ffn_conv_b[l]), fp),
            w_down=jnp.pad(w_down[l], ((0, fp - d_ff), (0, 0))).astype(BF16),
            ln_ffn_g=row(ln_ffn_g[l]), ln_ffn_b=row(ln_ffn_b[l]),
        ))
    ln_in = (row(ln_in_g), row(ln_in_b))
    y_prompt = _trunk(x_prompt, mods_p, ln_in, layers, alpha)
    y_sample = _trunk(x_sample, mods_s, ln_in, layers, alpha)
    return (y_prompt, y_sample)
```

```python
import functools
import math

import jax
import jax.numpy as jnp
from jax import lax
from jax.experimental import pallas as pl
from jax.experimental.pallas import tpu as pltpu

GRID_W = 64
NA_HEADS = 16
NA_HEAD_DIM = 64
NA_KH_MAX = 8
NA_KW = 16
CONV_K = 31
FFN_CONV_K = 3
N_MOD = 6
LN_EPS = 1e-5
NEG_INF = -1e9

LANES = 128
SUBLANES = 8
HALO = 16
MIX_ROWS = 256
CONV_CHUNK = 32
VMEM_LIMIT = 56 * 1024 * 1024

F32 = jnp.float32
BF16 = jnp.bfloat16


def _tile(n, pref, mult):
    t = min(pref, n)
    t -= t % mult
    while t > mult and n % t:
        t -= mult
    assert t >= mult and n % t == 0, (n, pref, mult)
    return t


def _ln(x, g, b):
    mu = jnp.mean(x, axis=-1, keepdims=True)
    xc = x - mu
    var = jnp.mean(xc * xc, axis=-1, keepdims=True)
    return xc * lax.rsqrt(var + LN_EPS) * g + b


def _params(*sem):
    return pltpu.CompilerParams(dimension_semantics=sem, vmem_limit_bytes=VMEM_LIMIT)


def _resident(shape):
    return pl.BlockSpec(shape, lambda *_: (0,) * len(shape), pipeline_mode=pl.Buffered(1))


def _mod_kernel(c_ref, w_ref, b_ref, o_ref):
    c = c_ref[...]
    a = c * jax.nn.sigmoid(c)
    o_ref[...] = jnp.dot(a, w_ref[...], preferred_element_type=F32) + b_ref[...]


def _modulation(c_all, w_mod, b_mod):
    n_layers, d, e = w_mod.shape
    r = c_all.shape[0]
    tn = _tile(e, 1024, LANES)
    return pl.pallas_call(
        _mod_kernel,
        grid=(n_layers, e // tn),
        in_specs=[
            pl.BlockSpec((r, d), lambda l, j: (0, 0)),
            pl.BlockSpec((None, d, tn), lambda l, j: (l, 0, j)),
            pl.BlockSpec((None, 1, tn), lambda l, j: (l, 0, j)),
        ],
        out_specs=pl.BlockSpec((None, r, tn), lambda l, j: (l, 0, j)),
        out_shape=jax.ShapeDtypeStruct((n_layers, r, e), F32),
        compiler_params=_params("arbitrary", "arbitrary"),
        name="modulation",
    )(c_all, w_mod, b_mod)


def _glu_kernel(*refs, first_layer):
    if first_layer:
        x_ref, mod_ref, g_ref, b_ref, wv_ref, wg_ref, bv_ref, bg_ref, x0_ref, h_ref, u_ref = refs
    else:
        x_ref, mod_ref, wv_ref, wg_ref, bv_ref, bg_ref, h_ref, u_ref = refs
    x = x_ref[...]
    if first_layer:
        x = _ln(x, g_ref[...], b_ref[...])
        x0_ref[...] = x
    h = (x * (1.0 + mod_ref[1:2, :]) + mod_ref[0:1, :]).astype(BF16)
    h_ref[...] = h
    value = jnp.dot(h, wv_ref[...], preferred_element_type=F32) + bv_ref[...]
    gate = jnp.dot(h, wg_ref[...], preferred_element_type=F32) + bg_ref[...]
    u_ref[...] = (value * jax.nn.sigmoid(gate)).astype(BF16)


def _glu_projection(x, mod, w_in, b_in, seq, d_attn, c, ln_in=None):
    n, d = x.shape
    assert (3 * d_attn) % c == 0
    value_blk = 3 * d_attn // c
    first = ln_in is not None
    tm = _tile(seq, 512, HALO)
    row = lambda i: (i, 0)
    in_specs = [pl.BlockSpec((tm, d), row),
                pl.BlockSpec((None, N_MOD, d), lambda i: (i * tm // seq, 0, 0))]
    args = [x, mod]
    if first:
        in_specs += [_resident(ln_in[0].shape)] * 2
        args += list(ln_in)
    single = dict(pipeline_mode=pl.Buffered(1))
    in_specs += [pl.BlockSpec((d, c), lambda i: (0, value_blk), **single),
                 pl.BlockSpec((d, c), lambda i: (0, value_blk + 1), **single),
                 pl.BlockSpec((1, c), lambda i: (0, value_blk), **single),
                 pl.BlockSpec((1, c), lambda i: (0, value_blk + 1), **single)]
    args += [w_in, w_in, b_in, b_in]
    out_specs = [pl.BlockSpec((tm, d), row), pl.BlockSpec((tm, c), row)]
    out_shape = [jax.ShapeDtypeStruct((n, d), BF16), jax.ShapeDtypeStruct((n, c), BF16)]
    if first:
        out_specs = [pl.BlockSpec((tm, d), row)] + out_specs
        out_shape = [jax.ShapeDtypeStruct((n, d), F32)] + out_shape
    return pl.pallas_call(
        functools.partial(_glu_kernel, first_layer=first),
        grid=(n // tm,),
        in_specs=in_specs,
        out_specs=out_specs,
        out_shape=out_shape,
        compiler_params=_params("arbitrary"),
        name="glu_projection",
    )(*args)


def _inproj_kernel(h_ref, w_ref, bias_ref, qkv_ref, gates_ref, *, n_qkv_tiles):
    j = pl.program_id(1)
    res = (jnp.dot(h_ref[...], w_ref[...], preferred_element_type=F32) + bias_ref[...]).astype(BF16)

    @pl.when(j < n_qkv_tiles)
    def _():
        for p in range(qkv_ref.shape[0]):
            qkv_ref[p] = res[:, p * LANES:(p + 1) * LANES]

    @pl.when(j >= n_qkv_tiles)
    def _():
        gates_ref[...] = res


def _in_projection(h, w_in, b_in, seq, d_attn, c):
    n, d = h.shape
    e = w_in.shape[1]
    n_gate = e - 3 * d_attn - 2 * c
    tm = _tile(seq, 1024, HALO)
    tn = _tile(math.gcd(math.gcd(3 * d_attn, 2 * c), n_gate), 1024, LANES)
    nq = 3 * d_attn // tn
    skip = 2 * c // tn
    col = lambda i, j: (0, jnp.where(j < nq, j, j + skip))
    return pl.pallas_call(
        functools.partial(_inproj_kernel, n_qkv_tiles=nq),
        grid=(n // tm, nq + n_gate // tn),
        in_specs=[
            pl.BlockSpec((tm, d), lambda i, j: (i, 0)),
            pl.BlockSpec((d, tn), col),
            pl.BlockSpec((1, tn), col),
        ],
        out_specs=[
            pl.BlockSpec((tn // LANES, tm, LANES), lambda i, j: (jnp.minimum(j, nq - 1), i, 0)),
            pl.BlockSpec((tm, tn), lambda i, j: (i, jnp.maximum(j - nq, 0))),
        ],
        out_shape=[jax.ShapeDtypeStruct((3 * d_attn // LANES, n, LANES), BF16),
                   jax.ShapeDtypeStruct((n, n_gate), BF16)],
        compiler_params=_params("arbitrary", "arbitrary"),
        name="in_projection",
    )(h, w_in, b_in)


def _conv_kernel(u_ref, up_ref, un_ref, w_ref, cb_ref, g_ref, b_ref, o_ref, sh, conv, *, tiles_per_seq):
    tt = u_ref.shape[0]
    t = pl.program_id(0) % tiles_per_seq
    sh[0, 0:HALO, :] = jnp.where(t == 0, 0.0, up_ref[...].astype(F32))
    sh[0, HALO:HALO + tt, :] = u_ref[...].astype(F32)
    sh[0, HALO + tt:, :] = jnp.where(t == tiles_per_seq - 1, 0.0, un_ref[...].astype(F32))
    first_tap = HALO - CONV_K // 2
    span = tt + (first_tap + CONV_K - 1) // SUBLANES * SUBLANES
    for s in range(1, SUBLANES):
        sh[s, 0:span, :] = sh[0, s:s + span, :]

    def chunk(ci, carry):
        base = ci * CONV_CHUNK
        accs = [cb_ref[...]] * (CONV_CHUNK // SUBLANES)
        for k in range(CONV_K):
            tiles, phase = divmod(first_tap + k, SUBLANES)
            w_k = w_ref[k]
            for i in range(len(accs)):
                start = pl.multiple_of(base + (tiles + i) * SUBLANES, SUBLANES)
                accs[i] = accs[i] + w_k * sh[phase, pl.ds(start, SUBLANES), :]
        for i, acc in enumerate(accs):
            conv[pl.ds(pl.multiple_of(base + i * SUBLANES, SUBLANES), SUBLANES), :] = acc
        return carry

    lax.fori_loop(0, tt // CONV_CHUNK, chunk, 0)
    y = _ln(conv[...], g_ref[...], b_ref[...])
    o_ref[...] = (y * jax.nn.sigmoid(y)).astype(o_ref.dtype)


def _conv_branch(u, conv_w, conv_b, ln_g, ln_b, seq):
    n, c = u.shape
    tt = _tile(seq, 512, CONV_CHUNK)
    th = tt // HALO
    last_halo = n // HALO - 1
    vec = pl.BlockSpec((1, c), lambda i: (0, 0))
    return pl.pallas_call(
        functools.partial(_conv_kernel, tiles_per_seq=seq // tt),
        grid=(n // tt,),
        in_specs=[
            pl.BlockSpec((tt, c), lambda i: (i, 0)),
            pl.BlockSpec((HALO, c), lambda i: (jnp.maximum(i * th - 1, 0), 0)),
            pl.BlockSpec((HALO, c), lambda i: (jnp.minimum((i + 1) * th, last_halo), 0)),
            pl.BlockSpec((CONV_K, SUBLANES, c), lambda i: (0, 0, 0)),
            pl.BlockSpec((SUBLANES, c), lambda i: (0, 0)),
            vec, vec,
        ],
        out_specs=pl.BlockSpec((tt, c), lambda i: (i, 0)),
        out_shape=jax.ShapeDtypeStruct((n, c), BF16),
        scratch_shapes=[pltpu.VMEM((SUBLANES, tt + 2 * HALO, c), F32), pltpu.VMEM((tt, c), F32)],
        compiler_params=_params("arbitrary"),
        name="conv_branch",
    )(u, u, u, conv_w, conv_b, ln_g, ln_b)


def _attn_kernel(q_ref, k_ref, v_ref, bias_ref, o_ref, *, rows, rows_per_step):
    kh = NA_KH_MAX
    nk = kh * GRID_W
    step = pl.program_id(2)
    low_half = lax.broadcasted_iota(jnp.int32, (GRID_W, LANES), 1) < NA_HEAD_DIM
    kc = lax.broadcasted_iota(jnp.int32, (nk, LANES), 0) & (GRID_W - 1)
    qc = lax.broadcasted_iota(jnp.int32, (nk, LANES), 1) & (GRID_W - 1)
    c0 = jnp.clip(qc - NA_KW // 2, 0, GRID_W - NA_KW)
    mask = (kc >= c0) & (kc < c0 + NA_KW)
    scale = NA_HEAD_DIM ** -0.5

    for rr in range(rows_per_step):
        r = step * rows_per_step + rr
        r0 = jnp.clip(r - kh // 2, 0, rows - kh)
        shift = r0 - r + (NA_KH_MAX - 1)
        koff = pl.multiple_of(r0 * GRID_W, GRID_W)
        qoff = rr * GRID_W
        k_win = k_ref[pl.ds(koff, nk), :]
        v_win = v_ref[pl.ds(koff, nk), :]
        q = q_ref[pl.ds(qoff, GRID_W), :] * scale
        zero = jnp.zeros_like(q)
        q_blocks = jnp.concatenate([jnp.where(low_half, q, zero), jnp.where(low_half, zero, q)], axis=0)
        s = lax.dot_general(k_win, q_blocks, (((1,), (1,)), ((), ())), preferred_element_type=F32)
        bias = bias_ref[pl.ds(shift, kh)].reshape(nk, LANES)
        s = jnp.where(mask, s + bias, NEG_INF)
        m = jnp.max(s, axis=0, keepdims=True)
        p = jnp.exp(s - m)
        inv = 1.0 / jnp.sum(p, axis=0, keepdims=True)
        pv_t = lax.dot_general(v_win, p.astype(BF16), (((0,), (0,)), ((), ())), preferred_element_type=F32)
        pv = (pv_t * inv).T
        o_ref[pl.ds(qoff, GRID_W), :] = jnp.where(low_half, pv[:GRID_W], pv[GRID_W:]).astype(o_ref.dtype)


def _bias_table(rpb):
    heads, n_dr, _ = rpb.shape
    cols = jnp.arange(GRID_W)
    dc = jnp.clip(cols[:, None] - cols[None, :] + NA_KW - 1, 0, 2 * NA_KW - 2)
    t = rpb.reshape(heads // 2, 2, n_dr, -1)[:, :, :, dc]
    return t.transpose(0, 2, 3, 1, 4).reshape(heads // 2, n_dr, GRID_W, 2 * GRID_W)


def _attention(qkv, bias, batch, seq):
    planes, n, _ = qkv.shape
    pairs = planes // 3
    rows = seq // GRID_W
    assert seq % GRID_W == 0 and rows >= NA_KH_MAX
    rps = _tile(rows, 16, 1)
    tq = rps * GRID_W
    steps = rows // rps
    return pl.pallas_call(
        functools.partial(_attn_kernel, rows=rows, rows_per_step=rps),
        grid=(batch, pairs, steps),
        in_specs=[
            pl.BlockSpec((None, tq, LANES), lambda b, p, s: (p, b * steps + s, 0)),
            pl.BlockSpec((None, seq, LANES), lambda b, p, s: (pairs + p, b, 0)),
            pl.BlockSpec((None, seq, LANES), lambda b, p, s: (2 * pairs + p, b, 0)),
            pl.BlockSpec((None,) + bias.shape[1:], lambda b, p, s: (p, 0, 0, 0)),
        ],
        out_specs=pl.BlockSpec((tq, LANES), lambda b, p, s: (b * steps + s, p)),
        out_shape=jax.ShapeDtypeStruct((n, pairs * LANES), BF16),
        compiler_params=_params("arbitrary", "arbitrary", "arbitrary"),
        name="neighbourhood_attention",
    )(qkv, qkv, qkv, bias)


def _mix_kernel(attn_ref, cu_ref, ga_ref, gc_ref, x_ref, mod_ref, wa_ref, wc_ref, wo_ref, g_ref, b_ref,
                x1_ref, h2_ref, *, alpha):
    for start in range(0, x_ref.shape[0], MIX_ROWS):
        rows = slice(start, start + MIX_ROWS)
        y_a = jnp.dot(attn_ref[rows, :], wa_ref[...], preferred_element_type=F32)
        y_c = jnp.dot(cu_ref[rows, :], wc_ref[...], preferred_element_type=F32)
        y = (jax.nn.sigmoid(ga_ref[rows, :].astype(F32)) * y_a
             + jax.nn.sigmoid(gc_ref[rows, :].astype(F32)) * y_c)
        y = jnp.dot(y.astype(BF16), wo_ref[...], preferred_element_type=F32)
        x1 = _ln(alpha * x_ref[rows, :] + mod_ref[2:3, :] * y, g_ref[...], b_ref[...])
        x1_ref[rows, :] = x1
        h2_ref[rows, :] = (x1 * (1.0 + mod_ref[4:5, :]) + mod_ref[3:4, :]).astype(BF16)


def _mix(attn, cu, gates, x, mod, w_attn_proj, w_conv_proj, w_out, ln_g, ln_b, seq, alpha):
    n, d = x.shape
    c = cu.shape[1]
    da = attn.shape[1]
    tm = _tile(seq, 2 * MIX_ROWS, MIX_ROWS)
    row = lambda i: (i, 0)
    return pl.pallas_call(
        functools.partial(_mix_kernel, alpha=alpha),
        grid=(n // tm,),
        in_specs=[
            pl.BlockSpec((tm, da), row),
            pl.BlockSpec((tm, c), row),
            pl.BlockSpec((tm, d), lambda i: (i, 0)),
            pl.BlockSpec((tm, d), lambda i: (i, 1)),
            pl.BlockSpec((tm, d), row),
            pl.BlockSpec((None, N_MOD, d), lambda i: (i * tm // seq, 0, 0)),
            _resident(w_attn_proj.shape),
            _resident(w_conv_proj.shape),
            _resident(w_out.shape),
            _resident(ln_g.shape),
            _resident(ln_b.shape),
        ],
        out_specs=[pl.BlockSpec((tm, d), row), pl.BlockSpec((tm, d), row)],
        out_shape=[jax.ShapeDtypeStruct((n, d), F32), jax.ShapeDtypeStruct((n, d), BF16)],
        compiler_params=_params("arbitrary"),
        name="merge_out_projection",
    )(attn, cu, gates, gates, x, mod, w_attn_proj, w_conv_proj, w_out, ln_g, ln_b)


def _ffn_kernel(h_ref, hp_ref, hn_ref, x_ref, mod_ref, wa_ref, wb_ref, cw_ref, cb_ref, wd_ref, g_ref, b_ref,
                o_ref, h_ext, a_ext, acc, *, tiles_per_seq, alpha):
    tm = h_ref.shape[0]
    j = pl.program_id(1)
    t = pl.program_id(0) % tiles_per_seq

    @pl.when(j == 0)
    def _():
        h_ext[0:HALO, :] = jnp.where(t == 0, jnp.zeros_like(hp_ref), hp_ref[...])
        h_ext[HALO:HALO + tm, :] = h_ref[...]
        h_ext[HALO + tm:, :] = jnp.where(t == tiles_per_seq - 1, jnp.zeros_like(hn_ref), hn_ref[...])
        acc[...] = jnp.zeros_like(acc)

    a_ext[...] = jnp.dot(h_ext[...], wa_ref[...], preferred_element_type=F32)
    gate = jnp.dot(h_ref[...], wb_ref[...], preferred_element_type=F32)
    a = cb_ref[...]
    for k in range(FFN_CONV_K):
        start = HALO - FFN_CONV_K // 2 + k
        a = a + cw_ref[k:k + 1, :] * a_ext[start:start + tm, :]
    act = 0.5 * a * (1.0 + lax.erf(a * (2.0 ** -0.5))) * gate
    acc[...] += jnp.dot(act.astype(BF16), wd_ref[...], preferred_element_type=F32)

    @pl.when(j == pl.num_programs(1) - 1)
    def _():
        o_ref[...] = _ln(alpha * x_ref[...] + mod_ref[5:6, :] * acc[...], g_ref[...], b_ref[...])


def _ffn(h2, x1, mod, w_up_a, w_up_b, conv_w, conv_b, w_down, ln_g, ln_b, seq, alpha):
    n, d = x1.shape
    fp = w_down.shape[0]
    tm = _tile(seq, 512, HALO)
    tf = _tile(fp, 512, LANES)
    nf = fp // tf
    th = tm // HALO
    last_halo = n // HALO - 1
    row = lambda i, j: (i, 0)
    vec = pl.BlockSpec((1, d), lambda i, j: (0, 0))
    return pl.pallas_call(
        functools.partial(_ffn_kernel, tiles_per_seq=seq // tm, alpha=alpha),
        grid=(n // tm, nf),
        in_specs=[
            pl.BlockSpec((tm, d), row),
            pl.BlockSpec((HALO, d), lambda i, j: (jnp.maximum(i * th - 1, 0), 0)),
            pl.BlockSpec((HALO, d), lambda i, j: (jnp.minimum((i + 1) * th, last_halo), 0)),
            pl.BlockSpec((tm, d), row),
            pl.BlockSpec((None, N_MOD, d), lambda i, j: (i * tm // seq, 0, 0)),
            pl.BlockSpec((d, tf), lambda i, j: (0, j)),
            pl.BlockSpec((d, tf), lambda i, j: (0, j)),
            pl.BlockSpec((FFN_CONV_K, tf), lambda i, j: (0, j)),
            pl.BlockSpec((1, tf), lambda i, j: (0, j)),
            pl.BlockSpec((tf, d), lambda i, j: (j, 0)),
            vec, vec,
        ],
        out_specs=pl.BlockSpec((tm, d), row),
        out_shape=jax.ShapeDtypeStruct((n, d), F32),
        scratch_shapes=[pltpu.VMEM((tm + 2 * HALO, d), BF16),
                        pltpu.VMEM((tm + 2 * HALO, tf), F32),
                        pltpu.VMEM((tm, d), F32)],
        compiler_params=_params("arbitrary", "arbitrary"),
        name="conv_ffn",
    )(h2, h2, h2, x1, mod, w_up_a, w_up_b, conv_w, conv_b, w_down, ln_g, ln_b)


def _pad_cols(w, width):
    zeros = jnp.zeros(w.shape[:-1] + (width - w.shape[-1],), w.dtype)
    return jnp.concatenate([w, zeros], axis=-1)


def _trunk(x, mods, ln_in, layers, alpha):
    batch, seq, d = x.shape
    x = x.reshape(batch * seq, d)
    for l, p in enumerate(layers):
        mod = mods[l]
        d_attn, c = p["w_attn_proj"].shape[0], p["w_conv_proj"].shape[0]
        if l == 0:
            x, h, u = _glu_projection(x, mod, p["w_in"], p["b_in"], seq, d_attn, c, ln_in=ln_in)
        else:
            h, u = _glu_projection(x, mod, p["w_in"], p["b_in"], seq, d_attn, c)
        qkv, gates = _in_projection(h, p["w_in"], p["b_in"], seq, d_attn, c)
        attn = _attention(qkv, p["bias"], batch, seq)
        cu = _conv_branch(u, p["conv_w"], p["conv_b"], p["conv_ln_g"], p["conv_ln_b"], seq)
        x1, h2 = _mix(attn, cu, gates, x, mod, p["w_attn_proj"], p["w_conv_proj"], p["w_out"],
                      p["ln_mix_g"], p["ln_mix_b"], seq, alpha)
        x = _ffn(h2, x1, mod, p["w_up_a"], p["w_up_b"], p["ffn_conv_w"], p["ffn_conv_b"], p["w_down"],
                 p["ln_ffn_g"], p["ln_ffn_b"], seq, alpha)
    return x.reshape(batch, seq, d)


def kernel(x_prompt, x_sample, c_prompt, c_sample, ln_in_g, ln_in_b, w_mod, b_mod, w_in, b_in, na_rpb, w_attn_proj, conv_w, conv_b, conv_ln_g, conv_ln_b, w_conv_proj, w_out, ln_mix_g, ln_mix_b, w_up, ffn_conv_w, ffn_conv_b, w_down, ln_ffn_g, ln_ffn_b):
    depth, d, _ = w_mod.shape
    alpha = (2 * depth) ** 0.25
    d_ff = w_down.shape[1]
    fp = -(-d_ff // 512) * 512 if d_ff > 512 else -(-d_ff // LANES) * LANES

    nb_p, nb_s = c_prompt.shape[0], c_sample.shape[0]
    c_all = jnp.concatenate([c_prompt, c_sample], axis=0)
    mod_all = _modulation(c_all, w_mod, b_mod[:, None, :]).reshape(depth, nb_p + nb_s, N_MOD, d)
    mods_p = [mod_all[l, :nb_p] for l in range(depth)]
    mods_s = [mod_all[l, nb_p:] for l in range(depth)]

    row = lambda v: v[None, :]
    layers = []
    for l in range(depth):
        up_a, up_b = w_up[l, :, :d_ff], w_up[l, :, d_ff:]
        layers.append(dict(
            w_in=w_in[l].astype(BF16), b_in=row(b_in[l]),
            bias=_bias_table(na_rpb[l]),
            w_attn_proj=w_attn_proj[l].astype(BF16),
            conv_w=jnp.broadcast_to(conv_w[l][:, None, :], (conv_w.shape[1], SUBLANES, conv_w.shape[2])),
            conv_b=jnp.broadcast_to(conv_b[l][None, :], (SUBLANES, conv_b.shape[1])),
            conv_ln_g=row(conv_ln_g[l]), conv_ln_b=row(conv_ln_b[l]),
            w_conv_proj=w_conv_proj[l].astype(BF16),
            w_out=w_out[l].astype(BF16),
            ln_mix_g=row(ln_mix_g[l]), ln_mix_b=row(ln_mix_b[l]),
            w_up_a=_pad_cols(up_a.astype(BF16), fp), w_up_b=_pad_cols(up_b.astype(BF16), fp),
            ffn_conv_w=_pad_cols(ffn_conv_w[l], fp), ffn_conv_b=_pad_cols(row(ffn_conv_b[l]), fp),
            w_down=jnp.concatenate([w_down[l].astype(BF16), jnp.zeros((fp - d_ff, d), BF16)], axis=0),
            ln_ffn_g=row(ln_ffn_g[l]), ln_ffn_b=row(ln_ffn_b[l]),
        ))
    ln_in = (row(ln_in_g), row(ln_in_b))
    y_prompt = _trunk(x_prompt, mods_p, ln_in, layers, alpha)
    y_sample = _trunk(x_sample, mods_s, ln_in, layers, alpha)
    return (y_prompt, y_sample)
```

```python
import functools
import math

import jax
import jax.numpy as jnp
from jax import lax
from jax.experimental import pallas as pl
from jax.experimental.pallas import tpu as pltpu

GRID_W = 64
NA_HEADS = 16
NA_HEAD_DIM = 64
NA_KH_MAX = 8
NA_KW = 16
CONV_K = 31
FFN_CONV_K = 3
N_MOD = 6
LN_EPS = 1e-5
NEG_INF = -1e9

LANES = 128
SUBLANES = 8
HALO = 16
FFN_SLAB = 512
MIX_ROWS = 256
CONV_CHUNK = 32
VMEM_LIMIT = 56 * 1024 * 1024

F32 = jnp.float32
BF16 = jnp.bfloat16


def _tile(n, pref, mult):
    t = min(pref, n)
    t -= t % mult
    while t > mult and n % t:
        t -= mult
    assert t >= mult and n % t == 0, (n, pref, mult)
    return t


def _ln(x, g, b):
    mu = jnp.mean(x, axis=-1, keepdims=True)
    xc = x - mu
    var = jnp.mean(xc * xc, axis=-1, keepdims=True)
    return xc * lax.rsqrt(var + LN_EPS) * g + b


def _params(*sem):
    return pltpu.CompilerParams(dimension_semantics=sem, vmem_limit_bytes=VMEM_LIMIT)


def _resident(shape):
    return pl.BlockSpec(shape, lambda *_: (0,) * len(shape), pipeline_mode=pl.Buffered(1))


def _mod_kernel(c_ref, w_ref, b_ref, o_ref):
    c = c_ref[...]
    a = c * jax.nn.sigmoid(c)
    o_ref[...] = jnp.dot(a, w_ref[...], preferred_element_type=F32) + b_ref[...]


def _modulation(c_all, w_mod, b_mod):
    n_layers, d, e = w_mod.shape
    r = c_all.shape[0]
    tn = _tile(e, 1024, LANES)
    return pl.pallas_call(
        _mod_kernel,
        grid=(n_layers, e // tn),
        in_specs=[
            pl.BlockSpec((r, d), lambda l, j: (0, 0)),
            pl.BlockSpec((None, d, tn), lambda l, j: (l, 0, j)),
            pl.BlockSpec((None, 1, tn), lambda l, j: (l, 0, j)),
        ],
        out_specs=pl.BlockSpec((None, r, tn), lambda l, j: (l, 0, j)),
        out_shape=jax.ShapeDtypeStruct((n_layers, r, e), F32),
        compiler_params=_params("arbitrary", "arbitrary"),
        name="modulation",
    )(c_all, w_mod, b_mod)


def _glu_kernel(*refs, first_layer):
    if first_layer:
        x_ref, mod_ref, g_ref, b_ref, wv_ref, wg_ref, bv_ref, bg_ref, x0_ref, h_ref, u_ref = refs
    else:
        x_ref, mod_ref, wv_ref, wg_ref, bv_ref, bg_ref, h_ref, u_ref = refs
    x = x_ref[...]
    if first_layer:
        x = _ln(x, g_ref[...], b_ref[...])
        x0_ref[...] = x
    h = (x * (1.0 + mod_ref[1:2, :]) + mod_ref[0:1, :]).astype(BF16)
    h_ref[...] = h
    value = jnp.dot(h, wv_ref[...], preferred_element_type=F32) + bv_ref[...]
    gate = jnp.dot(h, wg_ref[...], preferred_element_type=F32) + bg_ref[...]
    u_ref[...] = (value * jax.nn.sigmoid(gate)).astype(BF16)


def _glu_projection(x, mod, w_in, b_in, seq, d_attn, c, ln_in=None):
    n, d = x.shape
    assert (3 * d_attn) % c == 0
    value_blk = 3 * d_attn // c
    first = ln_in is not None
    tm = _tile(seq, 512, HALO)
    row = lambda i: (i, 0)
    in_specs = [pl.BlockSpec((tm, d), row),
                pl.BlockSpec((None, N_MOD, d), lambda i: (i * tm // seq, 0, 0))]
    args = [x, mod]
    if first:
        in_specs += [_resident(ln_in[0].shape)] * 2
        args += list(ln_in)
    single = dict(pipeline_mode=pl.Buffered(1))
    in_specs += [pl.BlockSpec((d, c), lambda i: (0, value_blk), **single),
                 pl.BlockSpec((d, c), lambda i: (0, value_blk + 1), **single),
                 pl.BlockSpec((1, c), lambda i: (0, value_blk), **single),
                 pl.BlockSpec((1, c), lambda i: (0, value_blk + 1), **single)]
    args += [w_in, w_in, b_in, b_in]
    out_specs = [pl.BlockSpec((tm, d), row), pl.BlockSpec((tm, c), row)]
    out_shape = [jax.ShapeDtypeStruct((n, d), BF16), jax.ShapeDtypeStruct((n, c), BF16)]
    if first:
        out_specs = [pl.BlockSpec((tm, d), row)] + out_specs
        out_shape = [jax.ShapeDtypeStruct((n, d), F32)] + out_shape
    return pl.pallas_call(
        functools.partial(_glu_kernel, first_layer=first),
        grid=(n // tm,),
        in_specs=in_specs,
        out_specs=out_specs,
        out_shape=out_shape,
        compiler_params=_params("arbitrary"),
        name="glu_projection",
    )(*args)


def _inproj_kernel(h_ref, w_ref, bias_ref, qkv_ref, gates_ref, *, n_qkv_tiles):
    j = pl.program_id(1)
    res = (jnp.dot(h_ref[...], w_ref[...], preferred_element_type=F32) + bias_ref[...]).astype(BF16)

    @pl.when(j < n_qkv_tiles)
    def _():
        for p in range(qkv_ref.shape[0]):
            qkv_ref[p] = res[:, p * LANES:(p + 1) * LANES]

    @pl.when(j >= n_qkv_tiles)
    def _():
        gates_ref[...] = res


def _in_projection(h, w_in, b_in, seq, d_attn, c):
    n, d = h.shape
    e = w_in.shape[1]
    n_gate = e - 3 * d_attn - 2 * c
    tm = _tile(seq, 1024, HALO)
    tn = _tile(math.gcd(math.gcd(3 * d_attn, 2 * c), n_gate), 1024, LANES)
    nq = 3 * d_attn // tn
    skip = 2 * c // tn
    col = lambda i, j: (0, jnp.where(j < nq, j, j + skip))
    return pl.pallas_call(
        functools.partial(_inproj_kernel, n_qkv_tiles=nq),
        grid=(n // tm, nq + n_gate // tn),
        in_specs=[
            pl.BlockSpec((tm, d), lambda i, j: (i, 0)),
            pl.BlockSpec((d, tn), col),
            pl.BlockSpec((1, tn), col),
        ],
        out_specs=[
            pl.BlockSpec((tn // LANES, tm, LANES), lambda i, j: (jnp.minimum(j, nq - 1), i, 0)),
            pl.BlockSpec((tm, tn), lambda i, j: (i, jnp.maximum(j - nq, 0))),
        ],
        out_shape=[jax.ShapeDtypeStruct((3 * d_attn // LANES, n, LANES), BF16),
                   jax.ShapeDtypeStruct((n, n_gate), BF16)],
        compiler_params=_params("arbitrary", "arbitrary"),
        name="in_projection",
    )(h, w_in, b_in)


def _conv_kernel(u_ref, up_ref, un_ref, w_ref, cb_ref, g_ref, b_ref, o_ref, sh, conv, *, tiles_per_seq):
    tt = u_ref.shape[0]
    t = pl.program_id(0) % tiles_per_seq
    sh[0, 0:HALO, :] = jnp.where(t == 0, 0.0, up_ref[...].astype(F32))
    sh[0, HALO:HALO + tt, :] = u_ref[...].astype(F32)
    sh[0, HALO + tt:, :] = jnp.where(t == tiles_per_seq - 1, 0.0, un_ref[...].astype(F32))
    first_tap = HALO - CONV_K // 2
    span = tt + (first_tap + CONV_K - 1) // SUBLANES * SUBLANES
    for s in range(1, SUBLANES):
        sh[s, 0:span, :] = sh[0, s:s + span, :]

    def chunk(ci, carry):
        base = ci * CONV_CHUNK
        accs = [cb_ref[...]] * (CONV_CHUNK // SUBLANES)
        for k in range(CONV_K):
            tiles, phase = divmod(first_tap + k, SUBLANES)
            w_k = w_ref[k]
            for i in range(len(accs)):
                start = pl.multiple_of(base + (tiles + i) * SUBLANES, SUBLANES)
                accs[i] = accs[i] + w_k * sh[phase, pl.ds(start, SUBLANES), :]
        for i, acc in enumerate(accs):
            conv[pl.ds(pl.multiple_of(base + i * SUBLANES, SUBLANES), SUBLANES), :] = acc
        return carry

    lax.fori_loop(0, tt // CONV_CHUNK, chunk, 0)
    y = _ln(conv[...], g_ref[...], b_ref[...])
    o_ref[...] = (y * jax.nn.sigmoid(y)).astype(o_ref.dtype)


def _conv_branch(u, conv_w, conv_b, ln_g, ln_b, seq):
    n, c = u.shape
    tt = _tile(seq, 512, CONV_CHUNK)
    th = tt // HALO
    last_halo = n // HALO - 1
    vec = pl.BlockSpec((1, c), lambda i: (0, 0))
    return pl.pallas_call(
        functools.partial(_conv_kernel, tiles_per_seq=seq // tt),
        grid=(n // tt,),
        in_specs=[
            pl.BlockSpec((tt, c), lambda i: (i, 0)),
            pl.BlockSpec((HALO, c), lambda i: (jnp.maximum(i * th - 1, 0), 0)),
            pl.BlockSpec((HALO, c), lambda i: (jnp.minimum((i + 1) * th, last_halo), 0)),
            pl.BlockSpec((CONV_K, SUBLANES, c), lambda i: (0, 0, 0)),
            pl.BlockSpec((SUBLANES, c), lambda i: (0, 0)),
            vec, vec,
        ],
        out_specs=pl.BlockSpec((tt, c), lambda i: (i, 0)),
        out_shape=jax.ShapeDtypeStruct((n, c), BF16),
        scratch_shapes=[pltpu.VMEM((SUBLANES, tt + 2 * HALO, c), F32), pltpu.VMEM((tt, c), F32)],
        compiler_params=_params("arbitrary"),
        name="conv_branch",
    )(u, u, u, conv_w, conv_b, ln_g, ln_b)


def _attn_kernel(q_ref, k_ref, v_ref, bias_ref, o_ref, *, rows, rows_per_step):
    kh = NA_KH_MAX
    nk = kh * GRID_W
    step = pl.program_id(2)
    low_half = lax.broadcasted_iota(jnp.int32, (GRID_W, LANES), 1) < NA_HEAD_DIM
    kc = lax.broadcasted_iota(jnp.int32, (nk, LANES), 0) & (GRID_W - 1)
    qc = lax.broadcasted_iota(jnp.int32, (nk, LANES), 1) & (GRID_W - 1)
    c0 = jnp.clip(qc - NA_KW // 2, 0, GRID_W - NA_KW)
    mask = (kc >= c0) & (kc < c0 + NA_KW)
    scale = NA_HEAD_DIM ** -0.5

    for rr in range(rows_per_step):
        r = step * rows_per_step + rr
        r0 = jnp.clip(r - kh // 2, 0, rows - kh)
        shift = r0 - r + (NA_KH_MAX - 1)
        koff = pl.multiple_of(r0 * GRID_W, GRID_W)
        qoff = rr * GRID_W
        k_win = k_ref[pl.ds(koff, nk), :]
        v_win = v_ref[pl.ds(koff, nk), :]
        q = q_ref[pl.ds(qoff, GRID_W), :] * scale
        zero = jnp.zeros_like(q)
        q_blocks = jnp.concatenate([jnp.where(low_half, q, zero), jnp.where(low_half, zero, q)], axis=0)
        s = lax.dot_general(k_win, q_blocks, (((1,), (1,)), ((), ())), preferred_element_type=F32)
        bias = bias_ref[pl.ds(shift, kh)].reshape(nk, LANES)
        s = jnp.where(mask, s + bias, NEG_INF)
        m = jnp.max(s, axis=0, keepdims=True)
        p = jnp.exp(s - m)
        inv = 1.0 / jnp.sum(p, axis=0, keepdims=True)
        pv_t = lax.dot_general(v_win, p.astype(BF16), (((0,), (0,)), ((), ())), preferred_element_type=F32)
        pv = (pv_t * inv).T
        o_ref[pl.ds(qoff, GRID_W), :] = jnp.where(low_half, pv[:GRID_W], pv[GRID_W:]).astype(o_ref.dtype)


def _bias_table(rpb):
    heads, n_dr, _ = rpb.shape
    cols = jnp.arange(GRID_W)
    dc = jnp.clip(cols[:, None] - cols[None, :] + NA_KW - 1, 0, 2 * NA_KW - 2)
    t = rpb.reshape(heads // 2, 2, n_dr, -1)[:, :, :, dc]
    return t.transpose(0, 2, 3, 1, 4).reshape(heads // 2, n_dr, GRID_W, 2 * GRID_W)


def _attention(qkv, bias, batch, seq):
    planes, n, _ = qkv.shape
    pairs = planes // 3
    rows = seq // GRID_W
    assert seq % GRID_W == 0 and rows >= NA_KH_MAX
    rps = _tile(rows, 64, 1)
    tq = rps * GRID_W
    steps = rows // rps
    return pl.pallas_call(
        functools.partial(_attn_kernel, rows=rows, rows_per_step=rps),
        grid=(batch, pairs, steps),
        in_specs=[
            pl.BlockSpec((None, tq, LANES), lambda b, p, s: (p, b * steps + s, 0)),
            pl.BlockSpec((None, seq, LANES), lambda b, p, s: (pairs + p, b, 0)),
            pl.BlockSpec((None, seq, LANES), lambda b, p, s: (2 * pairs + p, b, 0)),
            pl.BlockSpec((None,) + bias.shape[1:], lambda b, p, s: (p, 0, 0, 0)),
        ],
        out_specs=pl.BlockSpec((tq, LANES), lambda b, p, s: (b * steps + s, p)),
        out_shape=jax.ShapeDtypeStruct((n, pairs * LANES), BF16),
        compiler_params=_params("arbitrary", "arbitrary", "arbitrary"),
        name="neighbourhood_attention",
    )(qkv, qkv, qkv, bias)


def _mix_kernel(attn_ref, cu_ref, ga_ref, gc_ref, x_ref, mod_ref, wa_ref, wc_ref, wo_ref, g_ref, b_ref,
                x1_ref, h2_ref, *, alpha):
    for start in range(0, x_ref.shape[0], MIX_ROWS):
        rows = slice(start, start + MIX_ROWS)
        y_a = jnp.dot(attn_ref[rows, :], wa_ref[...], preferred_element_type=F32)
        y_c = jnp.dot(cu_ref[rows, :], wc_ref[...], preferred_element_type=F32)
        y = (jax.nn.sigmoid(ga_ref[rows, :].astype(F32)) * y_a
             + jax.nn.sigmoid(gc_ref[rows, :].astype(F32)) * y_c)
        y = jnp.dot(y.astype(BF16), wo_ref[...], preferred_element_type=F32)
        x1 = _ln(alpha * x_ref[rows, :] + mod_ref[2:3, :] * y, g_ref[...], b_ref[...])
        x1_ref[rows, :] = x1
        h2_ref[rows, :] = (x1 * (1.0 + mod_ref[4:5, :]) + mod_ref[3:4, :]).astype(BF16)


def _mix(attn, cu, gates, x, mod, w_attn_proj, w_conv_proj, w_out, ln_g, ln_b, seq, alpha):
    n, d = x.shape
    c = cu.shape[1]
    da = attn.shape[1]
    tm = _tile(seq, 2 * MIX_ROWS, MIX_ROWS)
    row = lambda i: (i, 0)
    return pl.pallas_call(
        functools.partial(_mix_kernel, alpha=alpha),
        grid=(n // tm,),
        in_specs=[
            pl.BlockSpec((tm, da), row),
            pl.BlockSpec((tm, c), row),
            pl.BlockSpec((tm, d), lambda i: (i, 0)),
            pl.BlockSpec((tm, d), lambda i: (i, 1)),
            pl.BlockSpec((tm, d), row),
            pl.BlockSpec((None, N_MOD, d), lambda i: (i * tm // seq, 0, 0)),
            _resident(w_attn_proj.shape),
            _resident(w_conv_proj.shape),
            _resident(w_out.shape),
            _resident(ln_g.shape),
            _resident(ln_b.shape),
        ],
        out_specs=[pl.BlockSpec((tm, d), row), pl.BlockSpec((tm, d), row)],
        out_shape=[jax.ShapeDtypeStruct((n, d), F32), jax.ShapeDtypeStruct((n, d), BF16)],
        compiler_params=_params("arbitrary"),
        name="merge_out_projection",
    )(attn, cu, gates, gates, x, mod, w_attn_proj, w_conv_proj, w_out, ln_g, ln_b)


def _ffn_kernel(h_ref, hp_ref, hn_ref, x_ref, mod_ref, wa_ref, wb_ref, cw_ref, cb_ref, wd_ref, g_ref, b_ref,
                o_ref, h_ext, a_ext, acc, *, tiles_per_seq, alpha):
    tm = h_ref.shape[0]
    j = pl.program_id(1)
    t = pl.program_id(0) % tiles_per_seq

    @pl.when(j == 0)
    def _():
        h_ext[0:HALO, :] = jnp.where(t == 0, jnp.zeros_like(hp_ref), hp_ref[...])
        h_ext[HALO:HALO + tm, :] = h_ref[...]
        h_ext[HALO + tm:, :] = jnp.where(t == tiles_per_seq - 1, jnp.zeros_like(hn_ref), hn_ref[...])
        acc[...] = jnp.zeros_like(acc)

    a_ext[...] = jnp.dot(h_ext[...], wa_ref[...], preferred_element_type=F32)
    gate = jnp.dot(h_ref[...], wb_ref[...], preferred_element_type=F32)
    a = cb_ref[...]
    for k in range(FFN_CONV_K):
        start = HALO - FFN_CONV_K // 2 + k
        a = a + cw_ref[k:k + 1, :] * a_ext[start:start + tm, :]
    act = 0.5 * a * (1.0 + lax.erf(a * (2.0 ** -0.5))) * gate
    acc[...] += jnp.dot(act.astype(BF16), wd_ref[...], preferred_element_type=F32)

    @pl.when(j == pl.num_programs(1) - 1)
    def _():
        o_ref[...] = _ln(alpha * x_ref[...] + mod_ref[5:6, :] * acc[...], g_ref[...], b_ref[...])


def _ffn(h2, x1, mod, w_up_a, w_up_b, conv_w, conv_b, w_down, ln_g, ln_b, seq, alpha):
    n, d = x1.shape
    fp = w_down.shape[0]
    tm = _tile(seq, 512, HALO)
    tf = _tile(fp, FFN_SLAB, LANES)
    nf = fp // tf
    th = tm // HALO
    last_halo = n // HALO - 1
    row = lambda i, j: (i, 0)
    return pl.pallas_call(
        functools.partial(_ffn_kernel, tiles_per_seq=seq // tm, alpha=alpha),
        grid=(n // tm, nf),
        in_specs=[
            pl.BlockSpec((tm, d), row),
            pl.BlockSpec((HALO, d), lambda i, j: (jnp.maximum(i * th - 1, 0), 0)),
            pl.BlockSpec((HALO, d), lambda i, j: (jnp.minimum((i + 1) * th, last_halo), 0)),
            pl.BlockSpec((tm, d), row),
            pl.BlockSpec((None, N_MOD, d), lambda i, j: (i * tm // seq, 0, 0)),
            pl.BlockSpec((d, tf), lambda i, j: (0, j)),
            pl.BlockSpec((d, tf), lambda i, j: (0, j)),
            pl.BlockSpec((FFN_CONV_K, tf), lambda i, j: (0, j)),
            pl.BlockSpec((1, tf), lambda i, j: (0, j)),
            pl.BlockSpec((tf, d), lambda i, j: (j, 0)),
            _resident(ln_g.shape), _resident(ln_b.shape),
        ],
        out_specs=pl.BlockSpec((tm, d), row),
        out_shape=jax.ShapeDtypeStruct((n, d), F32),
        scratch_shapes=[pltpu.VMEM((tm + 2 * HALO, d), BF16),
                        pltpu.VMEM((tm + 2 * HALO, tf), F32),
                        pltpu.VMEM((tm, d), F32)],
        compiler_params=_params("arbitrary", "arbitrary"),
        name="conv_ffn",
    )(h2, h2, h2, x1, mod, w_up_a, w_up_b, conv_w, conv_b, w_down, ln_g, ln_b)


def _pad_to(w, width, axis):
    shape = list(w.shape)
    shape[axis] = width - w.shape[axis]
    return jnp.concatenate([w, jnp.zeros(shape, w.dtype)], axis=axis)


def _ffn_params(w_up, conv_w, conv_b, w_down):
    d_ff = w_down.shape[0]
    fp = -(-d_ff // FFN_SLAB) * FFN_SLAB if d_ff > FFN_SLAB else -(-d_ff // LANES) * LANES
    return dict(w_up_a=_pad_to(w_up[:, :d_ff], fp, 1), w_up_b=_pad_to(w_up[:, d_ff:], fp, 1),
                ffn_conv_w=_pad_to(conv_w, fp, 1), ffn_conv_b=_pad_to(conv_b[None, :], fp, 1),
                w_down=_pad_to(w_down, fp, 0))


def _trunk(x, mods, ln_in, layers, alpha):
    batch, seq, d = x.shape
    x = x.reshape(batch * seq, d)
    for l, p in enumerate(layers):
        mod = mods[l]
        d_attn, c = p["w_attn_proj"].shape[0], p["w_conv_proj"].shape[0]
        if l == 0:
            x, h, u = _glu_projection(x, mod, p["w_in"], p["b_in"], seq, d_attn, c, ln_in=ln_in)
        else:
            h, u = _glu_projection(x, mod, p["w_in"], p["b_in"], seq, d_attn, c)
        qkv, gates = _in_projection(h, p["w_in"], p["b_in"], seq, d_attn, c)
        attn = _attention(qkv, p["bias"], batch, seq)
        cu = _conv_branch(u, p["conv_w"], p["conv_b"], p["conv_ln_g"], p["conv_ln_b"], seq)
        x1, h2 = _mix(attn, cu, gates, x, mod, p["w_attn_proj"], p["w_conv_proj"], p["w_out"],
                      p["ln_mix_g"], p["ln_mix_b"], seq, alpha)
        x = _ffn(h2, x1, mod, p["w_up_a"], p["w_up_b"], p["ffn_conv_w"], p["ffn_conv_b"], p["w_down"],
                 p["ln_ffn_g"], p["ln_ffn_b"], seq, alpha)
    return x.reshape(batch, seq, d)


def kernel(x_prompt, x_sample, c_prompt, c_sample, ln_in_g, ln_in_b, w_mod, b_mod, w_in, b_in, na_rpb, w_attn_proj, conv_w, conv_b, conv_ln_g, conv_ln_b, w_conv_proj, w_out, ln_mix_g, ln_mix_b, w_up, ffn_conv_w, ffn_conv_b, w_down, ln_ffn_g, ln_ffn_b):
    depth, d, _ = w_mod.shape
    alpha = (2 * depth) ** 0.25
    nb_p, nb_s = c_prompt.shape[0], c_sample.shape[0]
    c_all = jnp.concatenate([c_prompt, c_sample], axis=0)
    mod_all = _modulation(c_all, w_mod, b_mod[:, None, :]).reshape(depth, nb_p + nb_s, N_MOD, d)
    mods_p = [mod_all[l, :nb_p] for l in range(depth)]
    mods_s = [mod_all[l, nb_p:] for l in range(depth)]

    row = lambda v: v[None, :]
    w_up_bf16, w_down_bf16 = w_up.astype(BF16), w_down.astype(BF16)
    layers = []
    for l in range(depth):
        layers.append(dict(
            w_in=w_in[l].astype(BF16), b_in=row(b_in[l]),
            bias=_bias_table(na_rpb[l]),
            w_attn_proj=w_attn_proj[l].astype(BF16),
            conv_w=jnp.broadcast_to(conv_w[l][:, None, :], (conv_w.shape[1], SUBLANES, conv_w.shape[2])),
            conv_b=jnp.broadcast_to(conv_b[l][None, :], (SUBLANES, conv_b.shape[1])),
            conv_ln_g=row(conv_ln_g[l]), conv_ln_b=row(conv_ln_b[l]),
            w_conv_proj=w_conv_proj[l].astype(BF16),
            w_out=w_out[l].astype(BF16),
            ln_mix_g=row(ln_mix_g[l]), ln_mix_b=row(ln_mix_b[l]),
            ln_ffn_g=row(ln_ffn_g[l]), ln_ffn_b=row(ln_ffn_b[l]),
            **_ffn_params(w_up_bf16[l], ffn_conv_w[l], ffn_conv_b[l], w_down_bf16[l]),
        ))
    ln_in = (row(ln_in_g), row(ln_in_b))
    y_prompt = _trunk(x_prompt, mods_p, ln_in, layers, alpha)
    y_sample = _trunk(x_sample, mods_s, ln_in, layers, alpha)
    return (y_prompt, y_sample)
```

```python
import functools
import math

import jax
import jax.numpy as jnp
from jax import lax
from jax.experimental import pallas as pl
from jax.experimental.pallas import tpu as pltpu

GRID_W = 64
NA_HEADS = 16
NA_HEAD_DIM = 64
NA_KH_MAX = 8
NA_KW = 16
CONV_K = 31
FFN_CONV_K = 3
N_MOD = 6
LN_EPS = 1e-5
NEG_INF = -1e9

LANES = 128
SUBLANES = 8
HALO = 16
FFN_SLAB = 512
MIX_ROWS = 256
CONV_CHUNK = 32
VMEM_LIMIT = 56 * 1024 * 1024

F32 = jnp.float32
BF16 = jnp.bfloat16


def _tile(n, pref, mult):
    t = min(pref, n)
    t -= t % mult
    while t > mult and n % t:
        t -= mult
    assert t >= mult and n % t == 0, (n, pref, mult)
    return t


def _ln(x, g, b):
    mu = jnp.mean(x, axis=-1, keepdims=True)
    xc = x - mu
    var = jnp.mean(xc * xc, axis=-1, keepdims=True)
    return xc * lax.rsqrt(var + LN_EPS) * g + b


def _params(*sem):
    return pltpu.CompilerParams(dimension_semantics=sem, vmem_limit_bytes=VMEM_LIMIT)


def _resident(shape):
    return pl.BlockSpec(shape, lambda *_: (0,) * len(shape), pipeline_mode=pl.Buffered(1))


def _layer_block(layer, block, index, **kwargs):
    return pl.BlockSpec((None,) + tuple(block), lambda *grid: (layer,) + tuple(index(*grid)), **kwargs)


def _resident_layer(stacked, layer):
    return _layer_block(layer, stacked.shape[1:], lambda *_: (0,) * (stacked.ndim - 1),
                        pipeline_mode=pl.Buffered(1))


def _mod_kernel(c_ref, w_ref, b_ref, o_ref):
    c = c_ref[...]
    a = c * jax.nn.sigmoid(c)
    o_ref[...] = jnp.dot(a, w_ref[...], preferred_element_type=F32) + b_ref[...]


def _modulation(c_all, w_mod, b_mod):
    n_layers, d, e = w_mod.shape
    r = c_all.shape[0]
    tn = _tile(e, 1024, LANES)
    return pl.pallas_call(
        _mod_kernel,
        grid=(n_layers, e // tn),
        in_specs=[
            pl.BlockSpec((r, d), lambda l, j: (0, 0)),
            pl.BlockSpec((None, d, tn), lambda l, j: (l, 0, j)),
            pl.BlockSpec((None, 1, tn), lambda l, j: (l, 0, j)),
        ],
        out_specs=pl.BlockSpec((None, r, tn), lambda l, j: (l, 0, j)),
        out_shape=jax.ShapeDtypeStruct((n_layers, r, e), F32),
        compiler_params=_params("arbitrary", "arbitrary"),
        name="modulation",
    )(c_all, w_mod, b_mod)


def _glu_kernel(*refs, first_layer):
    if first_layer:
        x_ref, mod_ref, g_ref, b_ref, wv_ref, wg_ref, bv_ref, bg_ref, x0_ref, h_ref, u_ref = refs
    else:
        x_ref, mod_ref, wv_ref, wg_ref, bv_ref, bg_ref, h_ref, u_ref = refs
    x = x_ref[...]
    if first_layer:
        x = _ln(x, g_ref[...], b_ref[...])
        x0_ref[...] = x
    h = (x * (1.0 + mod_ref[1:2, :]) + mod_ref[0:1, :]).astype(BF16)
    h_ref[...] = h
    value = jnp.dot(h, wv_ref[...], preferred_element_type=F32) + bv_ref[...]
    gate = jnp.dot(h, wg_ref[...], preferred_element_type=F32) + bg_ref[...]
    u_ref[...] = (value * jax.nn.sigmoid(gate)).astype(BF16)


def _glu_projection(x, mod, w_in, layer, b_in, seq, d_attn, c, ln_in=None):
    n, d = x.shape
    assert (3 * d_attn) % c == 0
    value_blk = 3 * d_attn // c
    first = ln_in is not None
    tm = _tile(seq, 512, HALO)
    row = lambda i: (i, 0)
    in_specs = [pl.BlockSpec((tm, d), row),
                pl.BlockSpec((None, N_MOD, d), lambda i: (i * tm // seq, 0, 0))]
    args = [x, mod]
    if first:
        in_specs += [_resident(ln_in[0].shape)] * 2
        args += list(ln_in)
    single = dict(pipeline_mode=pl.Buffered(1))
    in_specs += [_layer_block(layer, (d, c), lambda i: (0, value_blk), **single),
                 _layer_block(layer, (d, c), lambda i: (0, value_blk + 1), **single),
                 pl.BlockSpec((1, c), lambda i: (0, value_blk), **single),
                 pl.BlockSpec((1, c), lambda i: (0, value_blk + 1), **single)]
    args += [w_in, w_in, b_in, b_in]
    out_specs = [pl.BlockSpec((tm, d), row), pl.BlockSpec((tm, c), row)]
    out_shape = [jax.ShapeDtypeStruct((n, d), BF16), jax.ShapeDtypeStruct((n, c), BF16)]
    if first:
        out_specs = [pl.BlockSpec((tm, d), row)] + out_specs
        out_shape = [jax.ShapeDtypeStruct((n, d), F32)] + out_shape
    return pl.pallas_call(
        functools.partial(_glu_kernel, first_layer=first),
        grid=(n // tm,),
        in_specs=in_specs,
        out_specs=out_specs,
        out_shape=out_shape,
        compiler_params=_params("arbitrary"),
        name="glu_projection",
    )(*args)


def _inproj_kernel(h_ref, w_ref, bias_ref, qkv_ref, gates_ref, *, n_qkv_tiles):
    j = pl.program_id(1)
    res = (jnp.dot(h_ref[...], w_ref[...], preferred_element_type=F32) + bias_ref[...]).astype(BF16)

    @pl.when(j < n_qkv_tiles)
    def _():
        for p in range(qkv_ref.shape[0]):
            qkv_ref[p] = res[:, p * LANES:(p + 1) * LANES]

    @pl.when(j >= n_qkv_tiles)
    def _():
        gates_ref[...] = res


def _in_projection(h, w_in, layer, b_in, seq, d_attn, c):
    n, d = h.shape
    e = w_in.shape[2]
    n_gate = e - 3 * d_attn - 2 * c
    tm = _tile(seq, 1024, HALO)
    tn = _tile(math.gcd(math.gcd(3 * d_attn, 2 * c), n_gate), 1024, LANES)
    nq = 3 * d_attn // tn
    skip = 2 * c // tn
    col = lambda i, j: (0, jnp.where(j < nq, j, j + skip))
    return pl.pallas_call(
        functools.partial(_inproj_kernel, n_qkv_tiles=nq),
        grid=(n // tm, nq + n_gate // tn),
        in_specs=[
            pl.BlockSpec((tm, d), lambda i, j: (i, 0)),
            _layer_block(layer, (d, tn), col),
            pl.BlockSpec((1, tn), col),
        ],
        out_specs=[
            pl.BlockSpec((tn // LANES, tm, LANES), lambda i, j: (jnp.minimum(j, nq - 1), i, 0)),
            pl.BlockSpec((tm, tn), lambda i, j: (i, jnp.maximum(j - nq, 0))),
        ],
        out_shape=[jax.ShapeDtypeStruct((3 * d_attn // LANES, n, LANES), BF16),
                   jax.ShapeDtypeStruct((n, n_gate), BF16)],
        compiler_params=_params("arbitrary", "arbitrary"),
        name="in_projection",
    )(h, w_in, b_in)


def _conv_kernel(u_ref, up_ref, un_ref, w_ref, cb_ref, g_ref, b_ref, o_ref, sh, conv, *, tiles_per_seq):
    tt = u_ref.shape[0]
    t = pl.program_id(0) % tiles_per_seq
    sh[0, 0:HALO, :] = jnp.where(t == 0, 0.0, up_ref[...].astype(F32))
    sh[0, HALO:HALO + tt, :] = u_ref[...].astype(F32)
    sh[0, HALO + tt:, :] = jnp.where(t == tiles_per_seq - 1, 0.0, un_ref[...].astype(F32))
    first_tap = HALO - CONV_K // 2
    span = tt + (first_tap + CONV_K - 1) // SUBLANES * SUBLANES
    for s in range(1, SUBLANES):
        sh[s, 0:span, :] = sh[0, s:s + span, :]

    def chunk(ci, carry):
        base = ci * CONV_CHUNK
        accs = [cb_ref[...]] * (CONV_CHUNK // SUBLANES)
        for k in range(CONV_K):
            tiles, phase = divmod(first_tap + k, SUBLANES)
            w_k = w_ref[k]
            for i in range(len(accs)):
                start = pl.multiple_of(base + (tiles + i) * SUBLANES, SUBLANES)
                accs[i] = accs[i] + w_k * sh[phase, pl.ds(start, SUBLANES), :]
        for i, acc in enumerate(accs):
            conv[pl.ds(pl.multiple_of(base + i * SUBLANES, SUBLANES), SUBLANES), :] = acc
        return carry

    lax.fori_loop(0, tt // CONV_CHUNK, chunk, 0)
    y = _ln(conv[...], g_ref[...], b_ref[...])
    o_ref[...] = (y * jax.nn.sigmoid(y)).astype(o_ref.dtype)


def _conv_branch(u, conv_w, conv_b, ln_g, ln_b, seq):
    n, c = u.shape
    tt = _tile(seq, 512, CONV_CHUNK)
    th = tt // HALO
    last_halo = n // HALO - 1
    vec = pl.BlockSpec((1, c), lambda i: (0, 0))
    return pl.pallas_call(
        functools.partial(_conv_kernel, tiles_per_seq=seq // tt),
        grid=(n // tt,),
        in_specs=[
            pl.BlockSpec((tt, c), lambda i: (i, 0)),
            pl.BlockSpec((HALO, c), lambda i: (jnp.maximum(i * th - 1, 0), 0)),
            pl.BlockSpec((HALO, c), lambda i: (jnp.minimum((i + 1) * th, last_halo), 0)),
            pl.BlockSpec((CONV_K, SUBLANES, c), lambda i: (0, 0, 0)),
            pl.BlockSpec((SUBLANES, c), lambda i: (0, 0)),
            vec, vec,
        ],
        out_specs=pl.BlockSpec((tt, c), lambda i: (i, 0)),
        out_shape=jax.ShapeDtypeStruct((n, c), BF16),
        scratch_shapes=[pltpu.VMEM((SUBLANES, tt + 2 * HALO, c), F32), pltpu.VMEM((tt, c), F32)],
        compiler_params=_params("arbitrary"),
        name="conv_branch",
    )(u, u, u, conv_w, conv_b, ln_g, ln_b)


def _attn_kernel(q_ref, k_ref, v_ref, bias_ref, o_ref, *, rows, rows_per_step):
    kh = NA_KH_MAX
    nk = kh * GRID_W
    step = pl.program_id(2)
    low_half = lax.broadcasted_iota(jnp.int32, (GRID_W, LANES), 1) < NA_HEAD_DIM
    kc = lax.broadcasted_iota(jnp.int32, (nk, LANES), 0) & (GRID_W - 1)
    qc = lax.broadcasted_iota(jnp.int32, (nk, LANES), 1) & (GRID_W - 1)
    c0 = jnp.clip(qc - NA_KW // 2, 0, GRID_W - NA_KW)
    mask = (kc >= c0) & (kc < c0 + NA_KW)
    scale = NA_HEAD_DIM ** -0.5

    for rr in range(rows_per_step):
        r = step * rows_per_step + rr
        r0 = jnp.clip(r - kh // 2, 0, rows - kh)
        shift = r0 - r + (NA_KH_MAX - 1)
        koff = pl.multiple_of(r0 * GRID_W, GRID_W)
        qoff = rr * GRID_W
        k_win = k_ref[pl.ds(koff, nk), :]
        v_win = v_ref[pl.ds(koff, nk), :]
        q = q_ref[pl.ds(qoff, GRID_W), :] * scale
        zero = jnp.zeros_like(q)
        q_blocks = jnp.concatenate([jnp.where(low_half, q, zero), jnp.where(low_half, zero, q)], axis=0)
        s = lax.dot_general(k_win, q_blocks, (((1,), (1,)), ((), ())), preferred_element_type=F32)
        bias = bias_ref[pl.ds(shift, kh)].reshape(nk, LANES)
        s = jnp.where(mask, s + bias, NEG_INF)
        m = jnp.max(s, axis=0, keepdims=True)
        p = jnp.exp(s - m)
        inv = 1.0 / jnp.sum(p, axis=0, keepdims=True)
        pv_t = lax.dot_general(v_win, p.astype(BF16), (((0,), (0,)), ((), ())), preferred_element_type=F32)
        pv = (pv_t * inv).T
        o_ref[pl.ds(qoff, GRID_W), :] = jnp.where(low_half, pv[:GRID_W], pv[GRID_W:]).astype(o_ref.dtype)


def _bias_table(rpb):
    heads, n_dr, _ = rpb.shape
    cols = jnp.arange(GRID_W)
    dc = jnp.clip(cols[:, None] - cols[None, :] + NA_KW - 1, 0, 2 * NA_KW - 2)
    t = rpb.reshape(heads // 2, 2, n_dr, -1)[:, :, :, dc]
    return t.transpose(0, 2, 3, 1, 4).reshape(heads // 2, n_dr, GRID_W, 2 * GRID_W)


def _attention(qkv, bias, batch, seq):
    planes, n, _ = qkv.shape
    pairs = planes // 3
    rows = seq // GRID_W
    assert seq % GRID_W == 0 and rows >= NA_KH_MAX
    rps = _tile(rows, 64, 1)
    tq = rps * GRID_W
    steps = rows // rps
    return pl.pallas_call(
        functools.partial(_attn_kernel, rows=rows, rows_per_step=rps),
        grid=(batch, pairs, steps),
        in_specs=[
            pl.BlockSpec((None, tq, LANES), lambda b, p, s: (p, b * steps + s, 0)),
            pl.BlockSpec((None, seq, LANES), lambda b, p, s: (pairs + p, b, 0)),
            pl.BlockSpec((None, seq, LANES), lambda b, p, s: (2 * pairs + p, b, 0)),
            pl.BlockSpec((None,) + bias.shape[1:], lambda b, p, s: (p, 0, 0, 0)),
        ],
        out_specs=pl.BlockSpec((tq, LANES), lambda b, p, s: (b * steps + s, p)),
        out_shape=jax.ShapeDtypeStruct((n, pairs * LANES), BF16),
        compiler_params=_params("arbitrary", "arbitrary", "arbitrary"),
        name="neighbourhood_attention",
    )(qkv, qkv, qkv, bias)


def _mix_kernel(attn_ref, cu_ref, ga_ref, gc_ref, x_ref, mod_ref, wa_ref, wc_ref, wo_ref, g_ref, b_ref,
                x1_ref, h2_ref, *, alpha):
    for start in range(0, x_ref.shape[0], MIX_ROWS):
        rows = slice(start, start + MIX_ROWS)
        y_a = jnp.dot(attn_ref[rows, :], wa_ref[...], preferred_element_type=F32)
        y_c = jnp.dot(cu_ref[rows, :], wc_ref[...], preferred_element_type=F32)
        y = (jax.nn.sigmoid(ga_ref[rows, :].astype(F32)) * y_a
             + jax.nn.sigmoid(gc_ref[rows, :].astype(F32)) * y_c)
        y = jnp.dot(y.astype(BF16), wo_ref[...], preferred_element_type=F32)
        x1 = _ln(alpha * x_ref[rows, :] + mod_ref[2:3, :] * y, g_ref[...], b_ref[...])
        x1_ref[rows, :] = x1
        h2_ref[rows, :] = (x1 * (1.0 + mod_ref[4:5, :]) + mod_ref[3:4, :]).astype(BF16)


def _mix(attn, cu, gates, x, mod, w_attn_proj, w_conv_proj, w_out, layer, ln_g, ln_b, seq, alpha):
    n, d = x.shape
    c = cu.shape[1]
    da = attn.shape[1]
    tm = _tile(seq, 2 * MIX_ROWS, MIX_ROWS)
    row = lambda i: (i, 0)
    return pl.pallas_call(
        functools.partial(_mix_kernel, alpha=alpha),
        grid=(n // tm,),
        in_specs=[
            pl.BlockSpec((tm, da), row),
            pl.BlockSpec((tm, c), row),
            pl.BlockSpec((tm, d), lambda i: (i, 0)),
            pl.BlockSpec((tm, d), lambda i: (i, 1)),
            pl.BlockSpec((tm, d), row),
            pl.BlockSpec((None, N_MOD, d), lambda i: (i * tm // seq, 0, 0)),
            _resident_layer(w_attn_proj, layer),
            _resident_layer(w_conv_proj, layer),
            _resident_layer(w_out, layer),
            _resident(ln_g.shape),
            _resident(ln_b.shape),
        ],
        out_specs=[pl.BlockSpec((tm, d), row), pl.BlockSpec((tm, d), row)],
        out_shape=[jax.ShapeDtypeStruct((n, d), F32), jax.ShapeDtypeStruct((n, d), BF16)],
        compiler_params=_params("arbitrary"),
        name="merge_out_projection",
    )(attn, cu, gates, gates, x, mod, w_attn_proj, w_conv_proj, w_out, ln_g, ln_b)


def _ffn_kernel(h_ref, hp_ref, hn_ref, x_ref, mod_ref, wa_ref, wb_ref, cw_ref, cb_ref, wd_ref, g_ref, b_ref,
                o_ref, h_ext, a_ext, acc, *, tiles_per_seq, alpha):
    tm = h_ref.shape[0]
    j = pl.program_id(1)
    t = pl.program_id(0) % tiles_per_seq

    @pl.when(j == 0)
    def _():
        h_ext[0:HALO, :] = jnp.where(t == 0, jnp.zeros_like(hp_ref), hp_ref[...])
        h_ext[HALO:HALO + tm, :] = h_ref[...]
        h_ext[HALO + tm:, :] = jnp.where(t == tiles_per_seq - 1, jnp.zeros_like(hn_ref), hn_ref[...])
        acc[...] = jnp.zeros_like(acc)

    a_ext[...] = jnp.dot(h_ext[...], wa_ref[...], preferred_element_type=F32)
    gate = jnp.dot(h_ref[...], wb_ref[...], preferred_element_type=F32)
    a = cb_ref[...]
    for k in range(FFN_CONV_K):
        start = HALO - FFN_CONV_K // 2 + k
        a = a + cw_ref[k:k + 1, :] * a_ext[start:start + tm, :]
    act = 0.5 * a * (1.0 + lax.erf(a * (2.0 ** -0.5))) * gate
    acc[...] += jnp.dot(act.astype(BF16), wd_ref[...], preferred_element_type=F32)

    @pl.when(j == pl.num_programs(1) - 1)
    def _():
        o_ref[...] = _ln(alpha * x_ref[...] + mod_ref[5:6, :] * acc[...], g_ref[...], b_ref[...])


def _ffn(h2, x1, mod, w_up_a, w_up_b, w_down, layer, conv_w, conv_b, ln_g, ln_b, seq, alpha):
    n, d = x1.shape
    fp = w_down.shape[1]
    tm = _tile(seq, 512, HALO)
    tf = _tile(fp, FFN_SLAB, LANES)
    nf = fp // tf
    th = tm // HALO
    last_halo = n // HALO - 1
    row = lambda i, j: (i, 0)
    return pl.pallas_call(
        functools.partial(_ffn_kernel, tiles_per_seq=seq // tm, alpha=alpha),
        grid=(n // tm, nf),
        in_specs=[
            pl.BlockSpec((tm, d), row),
            pl.BlockSpec((HALO, d), lambda i, j: (jnp.maximum(i * th - 1, 0), 0)),
            pl.BlockSpec((HALO, d), lambda i, j: (jnp.minimum((i + 1) * th, last_halo), 0)),
            pl.BlockSpec((tm, d), row),
            pl.BlockSpec((None, N_MOD, d), lambda i, j: (i * tm // seq, 0, 0)),
            _layer_block(layer, (d, tf), lambda i, j: (0, j)),
            _layer_block(layer, (d, tf), lambda i, j: (0, j)),
            pl.BlockSpec((FFN_CONV_K, tf), lambda i, j: (0, j)),
            pl.BlockSpec((1, tf), lambda i, j: (0, j)),
            _layer_block(layer, (tf, d), lambda i, j: (j, 0)),
            _resident(ln_g.shape), _resident(ln_b.shape),
        ],
        out_specs=pl.BlockSpec((tm, d), row),
        out_shape=jax.ShapeDtypeStruct((n, d), F32),
        scratch_shapes=[pltpu.VMEM((tm + 2 * HALO, d), BF16),
                        pltpu.VMEM((tm + 2 * HALO, tf), F32),
                        pltpu.VMEM((tm, d), F32)],
        compiler_params=_params("arbitrary", "arbitrary"),
        name="conv_ffn",
    )(h2, h2, h2, x1, mod, w_up_a, w_up_b, conv_w, conv_b, w_down, ln_g, ln_b)


def _pad_to(w, width, axis):
    shape = list(w.shape)
    shape[axis] = width - w.shape[axis]
    return jnp.concatenate([w, jnp.zeros(shape, w.dtype)], axis=axis)


def _trunk(x, mods, ln_in, big, layers, alpha):
    batch, seq, d = x.shape
    x = x.reshape(batch * seq, d)
    d_attn, c = big["w_attn_proj"].shape[1], big["w_conv_proj"].shape[1]
    for l, p in enumerate(layers):
        mod = mods[l]
        if l == 0:
            x, h, u = _glu_projection(x, mod, big["w_in"], l, p["b_in"], seq, d_attn, c, ln_in=ln_in)
        else:
            h, u = _glu_projection(x, mod, big["w_in"], l, p["b_in"], seq, d_attn, c)
        qkv, gates = _in_projection(h, big["w_in"], l, p["b_in"], seq, d_attn, c)
        attn = _attention(qkv, p["bias"], batch, seq)
        cu = _conv_branch(u, p["conv_w"], p["conv_b"], p["conv_ln_g"], p["conv_ln_b"], seq)
        x1, h2 = _mix(attn, cu, gates, x, mod, big["w_attn_proj"], big["w_conv_proj"], big["w_out"], l,
                      p["ln_mix_g"], p["ln_mix_b"], seq, alpha)
        x = _ffn(h2, x1, mod, big["w_up_a"], big["w_up_b"], big["w_down"], l,
                 p["ffn_conv_w"], p["ffn_conv_b"], p["ln_ffn_g"], p["ln_ffn_b"], seq, alpha)
    return x.reshape(batch, seq, d)


def kernel(x_prompt, x_sample, c_prompt, c_sample, ln_in_g, ln_in_b, w_mod, b_mod, w_in, b_in, na_rpb, w_attn_proj, conv_w, conv_b, conv_ln_g, conv_ln_b, w_conv_proj, w_out, ln_mix_g, ln_mix_b, w_up, ffn_conv_w, ffn_conv_b, w_down, ln_ffn_g, ln_ffn_b):
    depth, d, _ = w_mod.shape
    alpha = (2 * depth) ** 0.25
    nb_p, nb_s = c_prompt.shape[0], c_sample.shape[0]
    c_all = jnp.concatenate([c_prompt, c_sample], axis=0)
    mod_all = _modulation(c_all, w_mod, b_mod[:, None, :]).reshape(depth, nb_p + nb_s, N_MOD, d)
    mods_p = [mod_all[l, :nb_p] for l in range(depth)]
    mods_s = [mod_all[l, nb_p:] for l in range(depth)]

    d_ff = w_down.shape[1]
    fp = -(-d_ff // FFN_SLAB) * FFN_SLAB if d_ff > FFN_SLAB else -(-d_ff // LANES) * LANES
    w_up_bf16 = w_up.astype(BF16)
    big = dict(
        w_in=w_in.astype(BF16), w_attn_proj=w_attn_proj.astype(BF16), w_conv_proj=w_conv_proj.astype(BF16),
        w_out=w_out.astype(BF16),
        w_up_a=_pad_to(w_up_bf16[:, :, :d_ff], fp, 2), w_up_b=_pad_to(w_up_bf16[:, :, d_ff:], fp, 2),
        w_down=_pad_to(w_down.astype(BF16), fp, 1))

    row = lambda v: v[None, :]
    layers = []
    for l in range(depth):
        layers.append(dict(
            b_in=row(b_in[l]),
            bias=_bias_table(na_rpb[l]),
            conv_w=jnp.broadcast_to(conv_w[l][:, None, :], (conv_w.shape[1], SUBLANES, conv_w.shape[2])),
            conv_b=jnp.broadcast_to(conv_b[l][None, :], (SUBLANES, conv_b.shape[1])),
            conv_ln_g=row(conv_ln_g[l]), conv_ln_b=row(conv_ln_b[l]),
            ln_mix_g=row(ln_mix_g[l]), ln_mix_b=row(ln_mix_b[l]),
            ffn_conv_w=_pad_to(ffn_conv_w[l], fp, 1), ffn_conv_b=_pad_to(row(ffn_conv_b[l]), fp, 1),
            ln_ffn_g=row(ln_ffn_g[l]), ln_ffn_b=row(ln_ffn_b[l]),
        ))
    ln_in = (row(ln_in_g), row(ln_in_b))
    y_prompt = _trunk(x_prompt, mods_p, ln_in, big, layers, alpha)
    y_sample = _trunk(x_sample, mods_s, ln_in, big, layers, alpha)
    return (y_prompt, y_sample)
```

```python
import functools
import math

import jax
import jax.numpy as jnp
from jax import lax
from jax.experimental import pallas as pl
from jax.experimental.pallas import tpu as pltpu

GRID_W = 64
NA_HEAD_DIM = 64
NA_KH_MAX = 8
NA_KW = 16
CONV_K = 31
FFN_CONV_K = 3
N_MOD = 6
LN_EPS = 1e-5
NEG_INF = -1e9

LANES = 128
SUBLANES = 8
HALO = 16
VMEM_LIMIT = 56 * 1024 * 1024

TOKEN_TILE = 512
INPROJ_TILE = 1024
FFN_SLAB = 512
MIX_ROWS = 256
CONV_CHUNK = 32
ATTN_ROWS = 64

F32 = jnp.float32
BF16 = jnp.bfloat16


def _tile(n, pref, mult):
    t = min(pref, n)
    t -= t % mult
    while t > mult and n % t:
        t -= mult
    assert t >= mult and n % t == 0, (n, pref, mult)
    return t


def _ln(x, g, b):
    mu = jnp.mean(x, axis=-1, keepdims=True)
    xc = x - mu
    var = jnp.mean(xc * xc, axis=-1, keepdims=True)
    return xc * lax.rsqrt(var + LN_EPS) * g + b


def _params(*sem):
    return pltpu.CompilerParams(dimension_semantics=sem, vmem_limit_bytes=VMEM_LIMIT)


def _resident(shape):
    return pl.BlockSpec(shape, lambda *_: (0,) * len(shape), pipeline_mode=pl.Buffered(1))


def _layer_block(layer, block, index, **kwargs):
    return pl.BlockSpec((None,) + tuple(block), lambda *grid: (layer,) + tuple(index(*grid)), **kwargs)


def _resident_layer(stacked, layer):
    return _layer_block(layer, stacked.shape[1:], lambda *_: (0,) * (stacked.ndim - 1),
                        pipeline_mode=pl.Buffered(1))


def _mod_kernel(c_ref, w_ref, b_ref, o_ref):
    c = c_ref[...]
    a = c * jax.nn.sigmoid(c)
    o_ref[...] = jnp.dot(a, w_ref[...], preferred_element_type=F32) + b_ref[...]


def _modulation(c_all, w_mod, b_mod):
    n_layers, d, e = w_mod.shape
    r = c_all.shape[0]
    tn = _tile(e, INPROJ_TILE, LANES)
    return pl.pallas_call(
        _mod_kernel,
        grid=(n_layers, e // tn),
        in_specs=[
            pl.BlockSpec((r, d), lambda l, j: (0, 0)),
            pl.BlockSpec((None, d, tn), lambda l, j: (l, 0, j)),
            pl.BlockSpec((None, 1, tn), lambda l, j: (l, 0, j)),
        ],
        out_specs=pl.BlockSpec((None, r, tn), lambda l, j: (l, 0, j)),
        out_shape=jax.ShapeDtypeStruct((n_layers, r, e), F32),
        compiler_params=_params("arbitrary", "arbitrary"),
        name="modulation",
    )(c_all, w_mod, b_mod)


def _glu_kernel(*refs, first_layer):
    if first_layer:
        x_ref, mod_ref, g_ref, b_ref, wv_ref, wg_ref, bv_ref, bg_ref, x0_ref, h_ref, u_ref = refs
    else:
        x_ref, mod_ref, wv_ref, wg_ref, bv_ref, bg_ref, h_ref, u_ref = refs
    x = x_ref[...]
    if first_layer:
        x = _ln(x, g_ref[...], b_ref[...])
        x0_ref[...] = x
    h = (x * (1.0 + mod_ref[1:2, :]) + mod_ref[0:1, :]).astype(BF16)
    h_ref[...] = h
    value = jnp.dot(h, wv_ref[...], preferred_element_type=F32) + bv_ref[...]
    gate = jnp.dot(h, wg_ref[...], preferred_element_type=F32) + bg_ref[...]
    u_ref[...] = (value * jax.nn.sigmoid(gate)).astype(BF16)


def _glu_projection(x, mod, w_in, layer, b_in, seq, d_attn, c, ln_in=None):
    n, d = x.shape
    assert (3 * d_attn) % c == 0
    value_blk = 3 * d_attn // c
    first = ln_in is not None
    tm = _tile(seq, TOKEN_TILE, HALO)
    row = lambda i: (i, 0)
    in_specs = [pl.BlockSpec((tm, d), row),
                pl.BlockSpec((None, N_MOD, d), lambda i: (i * tm // seq, 0, 0))]
    args = [x, mod]
    if first:
        in_specs += [_resident(ln_in[0].shape)] * 2
        args += list(ln_in)
    single = dict(pipeline_mode=pl.Buffered(1))
    in_specs += [_layer_block(layer, (d, c), lambda i: (0, value_blk), **single),
                 _layer_block(layer, (d, c), lambda i: (0, value_blk + 1), **single),
                 pl.BlockSpec((1, c), lambda i: (0, value_blk), **single),
                 pl.BlockSpec((1, c), lambda i: (0, value_blk + 1), **single)]
    args += [w_in, w_in, b_in, b_in]
    out_specs = [pl.BlockSpec((tm, d), row), pl.BlockSpec((tm, c), row)]
    out_shape = [jax.ShapeDtypeStruct((n, d), BF16), jax.ShapeDtypeStruct((n, c), BF16)]
    if first:
        out_specs = [pl.BlockSpec((tm, d), row)] + out_specs
        out_shape = [jax.ShapeDtypeStruct((n, d), F32)] + out_shape
    return pl.pallas_call(
        functools.partial(_glu_kernel, first_layer=first),
        grid=(n // tm,),
        in_specs=in_specs,
        out_specs=out_specs,
        out_shape=out_shape,
        compiler_params=_params("arbitrary"),
        name="glu_projection",
    )(*args)


def _inproj_kernel(h_ref, w_ref, bias_ref, qkv_ref, gates_ref, *, n_qkv_tiles):
    j = pl.program_id(1)
    res = (jnp.dot(h_ref[...], w_ref[...], preferred_element_type=F32) + bias_ref[...]).astype(BF16)

    @pl.when(j < n_qkv_tiles)
    def _():
        for p in range(qkv_ref.shape[0]):
            qkv_ref[p] = res[:, p * LANES:(p + 1) * LANES]

    @pl.when(j >= n_qkv_tiles)
    def _():
        gates_ref[...] = res


def _in_projection(h, w_in, layer, b_in, seq, d_attn, c):
    n, d = h.shape
    e = w_in.shape[2]
    n_gate = e - 3 * d_attn - 2 * c
    tm = _tile(seq, INPROJ_TILE, HALO)
    tn = _tile(math.gcd(math.gcd(3 * d_attn, 2 * c), n_gate), INPROJ_TILE, LANES)
    nq = 3 * d_attn // tn
    skip = 2 * c // tn
    col = lambda i, j: (0, jnp.where(j < nq, j, j + skip))
    return pl.pallas_call(
        functools.partial(_inproj_kernel, n_qkv_tiles=nq),
        grid=(n // tm, nq + n_gate // tn),
        in_specs=[
            pl.BlockSpec((tm, d), lambda i, j: (i, 0)),
            _layer_block(layer, (d, tn), col),
            pl.BlockSpec((1, tn), col),
        ],
        out_specs=[
            pl.BlockSpec((tn // LANES, tm, LANES), lambda i, j: (jnp.minimum(j, nq - 1), i, 0)),
            pl.BlockSpec((tm, tn), lambda i, j: (i, jnp.maximum(j - nq, 0))),
        ],
        out_shape=[jax.ShapeDtypeStruct((3 * d_attn // LANES, n, LANES), BF16),
                   jax.ShapeDtypeStruct((n, n_gate), BF16)],
        compiler_params=_params("arbitrary", "arbitrary"),
        name="in_projection",
    )(h, w_in, b_in)


def _conv_kernel(u_ref, up_ref, un_ref, w_ref, cb_ref, g_ref, b_ref, o_ref, sh, conv, *, tiles_per_seq):
    tt = u_ref.shape[0]
    t = pl.program_id(0) % tiles_per_seq
    sh[0, 0:HALO, :] = jnp.where(t == 0, 0.0, up_ref[...].astype(F32))
    sh[0, HALO:HALO + tt, :] = u_ref[...].astype(F32)
    sh[0, HALO + tt:, :] = jnp.where(t == tiles_per_seq - 1, 0.0, un_ref[...].astype(F32))
    first_tap = HALO - CONV_K // 2
    span = tt + (first_tap + CONV_K - 1) // SUBLANES * SUBLANES
    for s in range(1, SUBLANES):
        sh[s, 0:span, :] = sh[0, s:s + span, :]

    def chunk(ci, carry):
        base = ci * CONV_CHUNK
        accs = [cb_ref[...]] * (CONV_CHUNK // SUBLANES)
        for k in range(CONV_K):
            tiles, phase = divmod(first_tap + k, SUBLANES)
            w_k = w_ref[k]
            for i in range(len(accs)):
                start = pl.multiple_of(base + (tiles + i) * SUBLANES, SUBLANES)
                accs[i] = accs[i] + w_k * sh[phase, pl.ds(start, SUBLANES), :]
        for i, acc in enumerate(accs):
            conv[pl.ds(pl.multiple_of(base + i * SUBLANES, SUBLANES), SUBLANES), :] = acc
        return carry

    lax.fori_loop(0, tt // CONV_CHUNK, chunk, 0)
    y = _ln(conv[...], g_ref[...], b_ref[...])
    o_ref[...] = (y * jax.nn.sigmoid(y)).astype(o_ref.dtype)


def _conv_branch(u, conv_w, conv_b, ln_g, ln_b, seq):
    n, c = u.shape
    tt = _tile(seq, TOKEN_TILE, CONV_CHUNK)
    th = tt // HALO
    last_halo = n // HALO - 1
    vec = pl.BlockSpec((1, c), lambda i: (0, 0))
    return pl.pallas_call(
        functools.partial(_conv_kernel, tiles_per_seq=seq // tt),
        grid=(n // tt,),
        in_specs=[
            pl.BlockSpec((tt, c), lambda i: (i, 0)),
            pl.BlockSpec((HALO, c), lambda i: (jnp.maximum(i * th - 1, 0), 0)),
            pl.BlockSpec((HALO, c), lambda i: (jnp.minimum((i + 1) * th, last_halo), 0)),
            pl.BlockSpec((CONV_K, SUBLANES, c), lambda i: (0, 0, 0)),
            pl.BlockSpec((SUBLANES, c), lambda i: (0, 0)),
            vec, vec,
        ],
        out_specs=pl.BlockSpec((tt, c), lambda i: (i, 0)),
        out_shape=jax.ShapeDtypeStruct((n, c), BF16),
        scratch_shapes=[pltpu.VMEM((SUBLANES, tt + 2 * HALO, c), F32), pltpu.VMEM((tt, c), F32)],
        compiler_params=_params("arbitrary"),
        name="conv_branch",
    )(u, u, u, conv_w, conv_b, ln_g, ln_b)


def _attn_kernel(q_ref, k_ref, v_ref, bias_ref, o_ref, *, rows, rows_per_step):
    kh = NA_KH_MAX
    nk = kh * GRID_W
    step = pl.program_id(2)
    low_half = lax.broadcasted_iota(jnp.int32, (GRID_W, LANES), 1) < NA_HEAD_DIM
    kc = lax.broadcasted_iota(jnp.int32, (nk, LANES), 0) & (GRID_W - 1)
    qc = lax.broadcasted_iota(jnp.int32, (nk, LANES), 1) & (GRID_W - 1)
    c0 = jnp.clip(qc - NA_KW // 2, 0, GRID_W - NA_KW)
    mask = (kc >= c0) & (kc < c0 + NA_KW)
    scale = NA_HEAD_DIM ** -0.5

    for rr in range(rows_per_step):
        r = step * rows_per_step + rr
        r0 = jnp.clip(r - kh // 2, 0, rows - kh)
        shift = r0 - r + (NA_KH_MAX - 1)
        koff = pl.multiple_of(r0 * GRID_W, GRID_W)
        qoff = rr * GRID_W
        k_win = k_ref[pl.ds(koff, nk), :]
        v_win = v_ref[pl.ds(koff, nk), :]
        q = q_ref[pl.ds(qoff, GRID_W), :] * scale
        zero = jnp.zeros_like(q)
        q_blocks = jnp.concatenate([jnp.where(low_half, q, zero), jnp.where(low_half, zero, q)], axis=0)
        s = lax.dot_general(k_win, q_blocks, (((1,), (1,)), ((), ())), preferred_element_type=F32)
        bias = bias_ref[pl.ds(shift, kh)].reshape(nk, LANES)
        s = jnp.where(mask, s + bias, NEG_INF)
        m = jnp.max(s, axis=0, keepdims=True)
        p = jnp.exp(s - m)
        inv = 1.0 / jnp.sum(p, axis=0, keepdims=True)
        pv_t = lax.dot_general(v_win, p.astype(BF16), (((0,), (0,)), ((), ())), preferred_element_type=F32)
        pv = (pv_t * inv).T
        o_ref[pl.ds(qoff, GRID_W), :] = jnp.where(low_half, pv[:GRID_W], pv[GRID_W:]).astype(o_ref.dtype)


def _bias_table(rpb):
    depth, heads, n_dr, _ = rpb.shape
    cols = jnp.arange(GRID_W)
    dc = jnp.clip(cols[:, None] - cols[None, :] + NA_KW - 1, 0, 2 * NA_KW - 2)
    t = rpb.reshape(depth, heads // 2, 2, n_dr, -1)[..., dc]
    return t.transpose(0, 1, 3, 4, 2, 5).reshape(depth, heads // 2, n_dr, GRID_W, 2 * GRID_W)


def _attention(qkv, bias, layer, batch, seq):
    planes, n, _ = qkv.shape
    pairs = planes // 3
    rows = seq // GRID_W
    assert seq % GRID_W == 0 and rows >= NA_KH_MAX
    rps = _tile(rows, ATTN_ROWS, 1)
    tq = rps * GRID_W
    steps = rows // rps
    return pl.pallas_call(
        functools.partial(_attn_kernel, rows=rows, rows_per_step=rps),
        grid=(batch, pairs, steps),
        in_specs=[
            pl.BlockSpec((None, tq, LANES), lambda b, p, s: (p, b * steps + s, 0)),
            pl.BlockSpec((None, seq, LANES), lambda b, p, s: (pairs + p, b, 0)),
            pl.BlockSpec((None, seq, LANES), lambda b, p, s: (2 * pairs + p, b, 0)),
            _layer_block(layer, (None,) + bias.shape[2:], lambda b, p, s: (p, 0, 0, 0)),
        ],
        out_specs=pl.BlockSpec((tq, LANES), lambda b, p, s: (b * steps + s, p)),
        out_shape=jax.ShapeDtypeStruct((n, pairs * LANES), BF16),
        compiler_params=_params("arbitrary", "arbitrary", "arbitrary"),
        name="neighbourhood_attention",
    )(qkv, qkv, qkv, bias)


def _mix_kernel(attn_ref, cu_ref, ga_ref, gc_ref, x_ref, mod_ref, wa_ref, wc_ref, wo_ref, g_ref, b_ref,
                x1_ref, h2_ref, *, alpha):
    for start in range(0, x_ref.shape[0], MIX_ROWS):
        rows = slice(start, start + MIX_ROWS)
        y_a = jnp.dot(attn_ref[rows, :], wa_ref[...], preferred_element_type=F32)
        y_c = jnp.dot(cu_ref[rows, :], wc_ref[...], preferred_element_type=F32)
        y = (jax.nn.sigmoid(ga_ref[rows, :].astype(F32)) * y_a
             + jax.nn.sigmoid(gc_ref[rows, :].astype(F32)) * y_c)
        y = jnp.dot(y.astype(BF16), wo_ref[...], preferred_element_type=F32)
        x1 = _ln(alpha * x_ref[rows, :] + mod_ref[2:3, :] * y, g_ref[...], b_ref[...])
        x1_ref[rows, :] = x1
        h2_ref[rows, :] = (x1 * (1.0 + mod_ref[4:5, :]) + mod_ref[3:4, :]).astype(BF16)


def _mix(attn, cu, gates, x, mod, w_attn_proj, w_conv_proj, w_out, layer, ln_g, ln_b, seq, alpha):
    n, d = x.shape
    c = cu.shape[1]
    da = attn.shape[1]
    tm = _tile(seq, 2 * MIX_ROWS, MIX_ROWS)
    row = lambda i: (i, 0)
    return pl.pallas_call(
        functools.partial(_mix_kernel, alpha=alpha),
        grid=(n // tm,),
        in_specs=[
            pl.BlockSpec((tm, da), row),
            pl.BlockSpec((tm, c), row),
            pl.BlockSpec((tm, d), lambda i: (i, 0)),
            pl.BlockSpec((tm, d), lambda i: (i, 1)),
            pl.BlockSpec((tm, d), row),
            pl.BlockSpec((None, N_MOD, d), lambda i: (i * tm // seq, 0, 0)),
            _resident_layer(w_attn_proj, layer),
            _resident_layer(w_conv_proj, layer),
            _resident_layer(w_out, layer),
            _resident(ln_g.shape),
            _resident(ln_b.shape),
        ],
        out_specs=[pl.BlockSpec((tm, d), row), pl.BlockSpec((tm, d), row)],
        out_shape=[jax.ShapeDtypeStruct((n, d), F32), jax.ShapeDtypeStruct((n, d), BF16)],
        compiler_params=_params("arbitrary"),
        name="merge_out_projection",
    )(attn, cu, gates, gates, x, mod, w_attn_proj, w_conv_proj, w_out, ln_g, ln_b)


def _ffn_kernel(h_ref, hp_ref, hn_ref, x_ref, mod_ref, wa_ref, wb_ref, cw_ref, cb_ref, wd_ref, g_ref, b_ref,
                o_ref, h_ext, a_ext, acc, *, tiles_per_seq, alpha):
    tm = h_ref.shape[0]
    j = pl.program_id(1)
    t = pl.program_id(0) % tiles_per_seq

    @pl.when(j == 0)
    def _():
        h_ext[0:HALO, :] = jnp.where(t == 0, jnp.zeros_like(hp_ref), hp_ref[...])
        h_ext[HALO:HALO + tm, :] = h_ref[...]
        h_ext[HALO + tm:, :] = jnp.where(t == tiles_per_seq - 1, jnp.zeros_like(hn_ref), hn_ref[...])
        acc[...] = jnp.zeros_like(acc)

    a_ext[...] = jnp.dot(h_ext[...], wa_ref[...], preferred_element_type=F32)
    gate = jnp.dot(h_ref[...], wb_ref[...], preferred_element_type=F32)
    a = cb_ref[...]
    for k in range(FFN_CONV_K):
        start = HALO - FFN_CONV_K // 2 + k
        a = a + cw_ref[k:k + 1, :] * a_ext[start:start + tm, :]
    act = 0.5 * a * (1.0 + lax.erf(a * (2.0 ** -0.5))) * gate
    acc[...] += jnp.dot(act.astype(BF16), wd_ref[...], preferred_element_type=F32)

    @pl.when(j == pl.num_programs(1) - 1)
    def _():
        o_ref[...] = _ln(alpha * x_ref[...] + mod_ref[5:6, :] * acc[...], g_ref[...], b_ref[...])


def _ffn(h2, x1, mod, w_up_a, w_up_b, w_down, layer, conv_w, conv_b, ln_g, ln_b, seq, alpha):
    n, d = x1.shape
    fp = w_down.shape[1]
    tm = _tile(seq, TOKEN_TILE, HALO)
    tf = _tile(fp, FFN_SLAB, LANES)
    nf = fp // tf
    th = tm // HALO
    last_halo = n // HALO - 1
    row = lambda i, j: (i, 0)
    return pl.pallas_call(
        functools.partial(_ffn_kernel, tiles_per_seq=seq // tm, alpha=alpha),
        grid=(n // tm, nf),
        in_specs=[
            pl.BlockSpec((tm, d), row),
            pl.BlockSpec((HALO, d), lambda i, j: (jnp.maximum(i * th - 1, 0), 0)),
            pl.BlockSpec((HALO, d), lambda i, j: (jnp.minimum((i + 1) * th, last_halo), 0)),
            pl.BlockSpec((tm, d), row),
            pl.BlockSpec((None, N_MOD, d), lambda i, j: (i * tm // seq, 0, 0)),
            _layer_block(layer, (d, tf), lambda i, j: (0, j)),
            _layer_block(layer, (d, tf), lambda i, j: (0, j)),
            pl.BlockSpec((FFN_CONV_K, tf), lambda i, j: (0, j)),
            pl.BlockSpec((1, tf), lambda i, j: (0, j)),
            _layer_block(layer, (tf, d), lambda i, j: (j, 0)),
            _resident(ln_g.shape), _resident(ln_b.shape),
        ],
        out_specs=pl.BlockSpec((tm, d), row),
        out_shape=jax.ShapeDtypeStruct((n, d), F32),
        scratch_shapes=[pltpu.VMEM((tm + 2 * HALO, d), BF16),
                        pltpu.VMEM((tm + 2 * HALO, tf), F32),
                        pltpu.VMEM((tm, d), F32)],
        compiler_params=_params("arbitrary", "arbitrary"),
        name="conv_ffn",
    )(h2, h2, h2, x1, mod, w_up_a, w_up_b, conv_w, conv_b, w_down, ln_g, ln_b)


def _pad_to(w, width, axis):
    shape = list(w.shape)
    shape[axis] = width - w.shape[axis]
    return jnp.concatenate([w, jnp.zeros(shape, w.dtype)], axis=axis)


def _trunk(x, mods, ln_in, big, layers, alpha):
    batch, seq, d = x.shape
    x = x.reshape(batch * seq, d)
    d_attn, c = big["w_attn_proj"].shape[1], big["w_conv_proj"].shape[1]
    for l, p in enumerate(layers):
        mod = mods[l]
        if l == 0:
            x, h, u = _glu_projection(x, mod, big["w_in"], l, p["b_in"], seq, d_attn, c, ln_in=ln_in)
        else:
            h, u = _glu_projection(x, mod, big["w_in"], l, p["b_in"], seq, d_attn, c)
        qkv, gates = _in_projection(h, big["w_in"], l, p["b_in"], seq, d_attn, c)
        attn = _attention(qkv, big["attn_bias"], l, batch, seq)
        cu = _conv_branch(u, p["conv_w"], p["conv_b"], p["conv_ln_g"], p["conv_ln_b"], seq)
        x1, h2 = _mix(attn, cu, gates, x, mod, big["w_attn_proj"], big["w_conv_proj"], big["w_out"], l,
                      p["ln_mix_g"], p["ln_mix_b"], seq, alpha)
        x = _ffn(h2, x1, mod, big["w_up_a"], big["w_up_b"], big["w_down"], l,
                 p["ffn_conv_w"], p["ffn_conv_b"], p["ln_ffn_g"], p["ln_ffn_b"], seq, alpha)
    return x.reshape(batch, seq, d)


def kernel(x_prompt, x_sample, c_prompt, c_sample, ln_in_g, ln_in_b, w_mod, b_mod, w_in, b_in, na_rpb, w_attn_proj, conv_w, conv_b, conv_ln_g, conv_ln_b, w_conv_proj, w_out, ln_mix_g, ln_mix_b, w_up, ffn_conv_w, ffn_conv_b, w_down, ln_ffn_g, ln_ffn_b):
    depth, d, _ = w_mod.shape
    alpha = (2 * depth) ** 0.25
    nb_p, nb_s = c_prompt.shape[0], c_sample.shape[0]
    c_all = jnp.concatenate([c_prompt, c_sample], axis=0)
    mod_all = _modulation(c_all, w_mod, b_mod[:, None, :]).reshape(depth, nb_p + nb_s, N_MOD, d)
    mods_p = [mod_all[l, :nb_p] for l in range(depth)]
    mods_s = [mod_all[l, nb_p:] for l in range(depth)]

    d_ff = w_down.shape[1]
    fp = -(-d_ff // FFN_SLAB) * FFN_SLAB if d_ff > FFN_SLAB else -(-d_ff // LANES) * LANES
    w_up_bf16 = w_up.astype(BF16)
    big = dict(
        w_in=w_in.astype(BF16), w_attn_proj=w_attn_proj.astype(BF16), w_conv_proj=w_conv_proj.astype(BF16),
        w_out=w_out.astype(BF16),
        w_up_a=_pad_to(w_up_bf16[:, :, :d_ff], fp, 2), w_up_b=_pad_to(w_up_bf16[:, :, d_ff:], fp, 2),
        w_down=_pad_to(w_down.astype(BF16), fp, 1),
        attn_bias=_bias_table(na_rpb))

    row = lambda v: v[None, :]
    layers = []
    for l in range(depth):
        layers.append(dict(
            b_in=row(b_in[l]),
            conv_w=jnp.broadcast_to(conv_w[l][:, None, :], (conv_w.shape[1], SUBLANES, conv_w.shape[2])),
            conv_b=jnp.broadcast_to(conv_b[l][None, :], (SUBLANES, conv_b.shape[1])),
            conv_ln_g=row(conv_ln_g[l]), conv_ln_b=row(conv_ln_b[l]),
            ln_mix_g=row(ln_mix_g[l]), ln_mix_b=row(ln_mix_b[l]),
            ffn_conv_w=_pad_to(ffn_conv_w[l], fp, 1), ffn_conv_b=_pad_to(row(ffn_conv_b[l]), fp, 1),
            ln_ffn_g=row(ln_ffn_g[l]), ln_ffn_b=row(ln_ffn_b[l]),
        ))
    ln_in = (row(ln_in_g), row(ln_in_b))
    y_prompt = _trunk(x_prompt, mods_p, ln_in, big, layers, alpha)
    y_sample = _trunk(x_sample, mods_s, ln_in, big, layers, alpha)
    return (y_prompt, y_sample)
```

```python
import functools
import math

import jax
import jax.numpy as jnp
from jax import lax
from jax.experimental import pallas as pl
from jax.experimental.pallas import tpu as pltpu

GRID_W = 64
NA_HEAD_DIM = 64
NA_KH_MAX = 8
NA_KW = 16
CONV_K = 31
FFN_CONV_K = 3
N_MOD = 6
LN_EPS = 1e-5
NEG_INF = -1e9

LANES = 128
SUBLANES = 8
HALO = 16
VMEM_LIMIT = 56 * 1024 * 1024

TOKEN_TILE = 512
INPROJ_TILE = 1024
FFN_SLAB = 512
MIX_ROWS = 256
CONV_CHUNK = 32
ATTN_ROWS = 64

F32 = jnp.float32
BF16 = jnp.bfloat16


def _tile(n, pref, mult):
    t = min(pref, n)
    t -= t % mult
    while t > mult and n % t:
        t -= mult
    assert t >= mult and n % t == 0, (n, pref, mult)
    return t


def _ln(x, g, b):
    mu = jnp.mean(x, axis=-1, keepdims=True)
    xc = x - mu
    var = jnp.mean(xc * xc, axis=-1, keepdims=True)
    return xc * lax.rsqrt(var + LN_EPS) * g + b


def _params(*sem):
    return pltpu.CompilerParams(dimension_semantics=sem, vmem_limit_bytes=VMEM_LIMIT)


def _resident(shape):
    return pl.BlockSpec(shape, lambda *_: (0,) * len(shape), pipeline_mode=pl.Buffered(1))


def _layer_block(layer, block, index, **kwargs):
    return pl.BlockSpec((None,) + tuple(block), lambda *grid: (layer,) + tuple(index(*grid)), **kwargs)


def _resident_layer(stacked, layer):
    return _layer_block(layer, stacked.shape[1:], lambda *_: (0,) * (stacked.ndim - 1),
                        pipeline_mode=pl.Buffered(1))


def _mod_kernel(c_ref, w_ref, b_ref, o_ref):
    c = c_ref[...]
    a = c * jax.nn.sigmoid(c)
    o_ref[...] = jnp.dot(a, w_ref[...], preferred_element_type=F32) + b_ref[...]


def _modulation(c_all, w_mod, b_mod):
    n_layers, d, e = w_mod.shape
    r = c_all.shape[0]
    tn = _tile(e, INPROJ_TILE, LANES)
    return pl.pallas_call(
        _mod_kernel,
        grid=(n_layers, e // tn),
        in_specs=[
            pl.BlockSpec((r, d), lambda l, j: (0, 0)),
            pl.BlockSpec((None, d, tn), lambda l, j: (l, 0, j)),
            pl.BlockSpec((None, 1, tn), lambda l, j: (l, 0, j)),
        ],
        out_specs=pl.BlockSpec((None, r, tn), lambda l, j: (l, 0, j)),
        out_shape=jax.ShapeDtypeStruct((n_layers, r, e), F32),
        compiler_params=_params("arbitrary", "arbitrary"),
        name="modulation",
    )(c_all, w_mod, b_mod)


def _glu_kernel(*refs, first_layer):
    if first_layer:
        x_ref, mod_ref, g_ref, b_ref, wv_ref, wg_ref, bv_ref, bg_ref, x0_ref, h_ref, u_ref = refs
    else:
        x_ref, mod_ref, wv_ref, wg_ref, bv_ref, bg_ref, h_ref, u_ref = refs
    x = x_ref[...]
    if first_layer:
        x = _ln(x, g_ref[...], b_ref[...])
        x0_ref[...] = x
    h = (x * (1.0 + mod_ref[1:2, :]) + mod_ref[0:1, :]).astype(BF16)
    h_ref[...] = h
    value = jnp.dot(h, wv_ref[...], preferred_element_type=F32) + bv_ref[...]
    gate = jnp.dot(h, wg_ref[...], preferred_element_type=F32) + bg_ref[...]
    u_ref[...] = (value * jax.nn.sigmoid(gate)).astype(BF16)


def _glu_projection(x, mod, w_in, layer, b_in, seq, d_attn, c, ln_in=None):
    n, d = x.shape
    assert (3 * d_attn) % c == 0
    value_blk = 3 * d_attn // c
    first = ln_in is not None
    tm = _tile(seq, TOKEN_TILE, HALO)
    row = lambda i: (i, 0)
    in_specs = [pl.BlockSpec((tm, d), row),
                pl.BlockSpec((None, N_MOD, d), lambda i: (i * tm // seq, 0, 0))]
    args = [x, mod]
    if first:
        in_specs += [_resident(ln_in[0].shape)] * 2
        args += list(ln_in)
    single = dict(pipeline_mode=pl.Buffered(1))
    in_specs += [_layer_block(layer, (d, c), lambda i: (0, value_blk), **single),
                 _layer_block(layer, (d, c), lambda i: (0, value_blk + 1), **single),
                 pl.BlockSpec((1, c), lambda i: (0, value_blk), **single),
                 pl.BlockSpec((1, c), lambda i: (0, value_blk + 1), **single)]
    args += [w_in, w_in, b_in, b_in]
    out_specs = [pl.BlockSpec((tm, d), row), pl.BlockSpec((tm, c), row)]
    out_shape = [jax.ShapeDtypeStruct((n, d), BF16), jax.ShapeDtypeStruct((n, c), BF16)]
    if first:
        out_specs = [pl.BlockSpec((tm, d), row)] + out_specs
        out_shape = [jax.ShapeDtypeStruct((n, d), F32)] + out_shape
    return pl.pallas_call(
        functools.partial(_glu_kernel, first_layer=first),
        grid=(n // tm,),
        in_specs=in_specs,
        out_specs=out_specs,
        out_shape=out_shape,
        compiler_params=_params("arbitrary"),
        name="glu_projection",
    )(*args)


def _inproj_kernel(h_ref, w_ref, bias_ref, qkv_ref, gates_ref, *, n_qkv_tiles):
    j = pl.program_id(1)
    res = (jnp.dot(h_ref[...], w_ref[...], preferred_element_type=F32) + bias_ref[...]).astype(BF16)

    @pl.when(j < n_qkv_tiles)
    def _():
        for p in range(qkv_ref.shape[0]):
            qkv_ref[p] = res[:, p * LANES:(p + 1) * LANES]

    @pl.when(j >= n_qkv_tiles)
    def _():
        gates_ref[...] = res


def _in_projection(h, w_in, layer, b_in, seq, d_attn, c):
    n, d = h.shape
    e = w_in.shape[2]
    n_gate = e - 3 * d_attn - 2 * c
    tm = _tile(seq, INPROJ_TILE, HALO)
    tn = _tile(math.gcd(math.gcd(3 * d_attn, 2 * c), n_gate), INPROJ_TILE, LANES)
    nq = 3 * d_attn // tn
    skip = 2 * c // tn
    col = lambda i, j: (0, jnp.where(j < nq, j, j + skip))
    return pl.pallas_call(
        functools.partial(_inproj_kernel, n_qkv_tiles=nq),
        grid=(n // tm, nq + n_gate // tn),
        in_specs=[
            pl.BlockSpec((tm, d), lambda i, j: (i, 0)),
            _layer_block(layer, (d, tn), col),
            pl.BlockSpec((1, tn), col),
        ],
        out_specs=[
            pl.BlockSpec((tn // LANES, tm, LANES), lambda i, j: (jnp.minimum(j, nq - 1), i, 0)),
            pl.BlockSpec((tm, tn), lambda i, j: (i, jnp.maximum(j - nq, 0))),
        ],
        out_shape=[jax.ShapeDtypeStruct((3 * d_attn // LANES, n, LANES), BF16),
                   jax.ShapeDtypeStruct((n, n_gate), BF16)],
        compiler_params=_params("arbitrary", "arbitrary"),
        name="in_projection",
    )(h, w_in, b_in)


def _conv_kernel(u_ref, up_ref, un_ref, w_ref, cb_ref, g_ref, b_ref, o_ref, sh, conv, *, tiles_per_seq):
    tt = u_ref.shape[0]
    t = pl.program_id(0) % tiles_per_seq
    sh[0, 0:HALO, :] = jnp.where(t == 0, 0.0, up_ref[...].astype(F32))
    sh[0, HALO:HALO + tt, :] = u_ref[...].astype(F32)
    sh[0, HALO + tt:, :] = jnp.where(t == tiles_per_seq - 1, 0.0, un_ref[...].astype(F32))
    first_tap = HALO - CONV_K // 2
    span = tt + (first_tap + CONV_K - 1) // SUBLANES * SUBLANES
    for s in range(1, SUBLANES):
        sh[s, 0:span, :] = sh[0, s:s + span, :]

    def chunk(ci, carry):
        base = ci * CONV_CHUNK
        accs = [cb_ref[...]] * (CONV_CHUNK // SUBLANES)
        for k in range(CONV_K):
            tiles, phase = divmod(first_tap + k, SUBLANES)
            w_k = w_ref[k]
            for i in range(len(accs)):
                start = pl.multiple_of(base + (tiles + i) * SUBLANES, SUBLANES)
                accs[i] = accs[i] + w_k * sh[phase, pl.ds(start, SUBLANES), :]
        for i, acc in enumerate(accs):
            conv[pl.ds(pl.multiple_of(base + i * SUBLANES, SUBLANES), SUBLANES), :] = acc
        return carry

    lax.fori_loop(0, tt // CONV_CHUNK, chunk, 0)
    y = _ln(conv[...], g_ref[...], b_ref[...])
    o_ref[...] = (y * jax.nn.sigmoid(y)).astype(o_ref.dtype)


def _conv_branch(u, conv_w, conv_b, ln_g, ln_b, seq):
    n, c = u.shape
    tt = _tile(seq, TOKEN_TILE, CONV_CHUNK)
    th = tt // HALO
    last_halo = n // HALO - 1
    vec = pl.BlockSpec((1, c), lambda i: (0, 0))
    return pl.pallas_call(
        functools.partial(_conv_kernel, tiles_per_seq=seq // tt),
        grid=(n // tt,),
        in_specs=[
            pl.BlockSpec((tt, c), lambda i: (i, 0)),
            pl.BlockSpec((HALO, c), lambda i: (jnp.maximum(i * th - 1, 0), 0)),
            pl.BlockSpec((HALO, c), lambda i: (jnp.minimum((i + 1) * th, last_halo), 0)),
            pl.BlockSpec((CONV_K, SUBLANES, c), lambda i: (0, 0, 0)),
            pl.BlockSpec((SUBLANES, c), lambda i: (0, 0)),
            vec, vec,
        ],
        out_specs=pl.BlockSpec((tt, c), lambda i: (i, 0)),
        out_shape=jax.ShapeDtypeStruct((n, c), BF16),
        scratch_shapes=[pltpu.VMEM((SUBLANES, tt + 2 * HALO, c), F32), pltpu.VMEM((tt, c), F32)],
        compiler_params=_params("arbitrary"),
        name="conv_branch",
    )(u, u, u, conv_w, conv_b, ln_g, ln_b)


def _attn_kernel(q_ref, k_ref, v_ref, bias_ref, o_ref, *, rows, rows_per_step):
    kh = NA_KH_MAX
    nk = kh * GRID_W
    step = pl.program_id(2)
    low_half = lax.broadcasted_iota(jnp.int32, (GRID_W, LANES), 1) < NA_HEAD_DIM
    kc = lax.broadcasted_iota(jnp.int32, (nk, LANES), 0) & (GRID_W - 1)
    qc = lax.broadcasted_iota(jnp.int32, (nk, LANES), 1) & (GRID_W - 1)
    c0 = jnp.clip(qc - NA_KW // 2, 0, GRID_W - NA_KW)
    mask = (kc >= c0) & (kc < c0 + NA_KW)
    scale = NA_HEAD_DIM ** -0.5

    for rr in range(rows_per_step):
        r = step * rows_per_step + rr
        r0 = jnp.clip(r - kh // 2, 0, rows - kh)
        shift = r0 - r + (NA_KH_MAX - 1)
        koff = pl.multiple_of(r0 * GRID_W, GRID_W)
        qoff = rr * GRID_W
        k_win = k_ref[pl.ds(koff, nk), :]
        v_win = v_ref[pl.ds(koff, nk), :]
        q = q_ref[pl.ds(qoff, GRID_W), :] * scale
        zero = jnp.zeros_like(q)
        q_blocks = jnp.concatenate([jnp.where(low_half, q, zero), jnp.where(low_half, zero, q)], axis=0)
        s = lax.dot_general(k_win, q_blocks, (((1,), (1,)), ((), ())), preferred_element_type=F32)
        bias = bias_ref[pl.ds(shift, kh)].reshape(nk, LANES)
        s = jnp.where(mask, s + bias, NEG_INF)
        m = jnp.max(s, axis=0, keepdims=True)
        p = jnp.exp(s - m)
        inv = 1.0 / jnp.sum(p, axis=0, keepdims=True)
        pv_t = lax.dot_general(v_win, p.astype(BF16), (((0,), (0,)), ((), ())), preferred_element_type=F32)
        pv = (pv_t * inv).T
        o_ref[pl.ds(qoff, GRID_W), :] = jnp.where(low_half, pv[:GRID_W], pv[GRID_W:]).astype(o_ref.dtype)


def _bias_table(rpb):
    depth, heads, n_dr, _ = rpb.shape
    cols = jnp.arange(GRID_W)
    dc = jnp.clip(cols[:, None] - cols[None, :] + NA_KW - 1, 0, 2 * NA_KW - 2)
    t = rpb.reshape(depth, heads // 2, 2, n_dr, -1)[..., dc]
    return t.transpose(0, 1, 3, 4, 2, 5).reshape(depth, heads // 2, n_dr, GRID_W, 2 * GRID_W)


def _attention(qkv, bias, layer, batch, seq):
    planes, n, _ = qkv.shape
    pairs = planes // 3
    rows = seq // GRID_W
    assert seq % GRID_W == 0 and rows >= NA_KH_MAX
    rps = _tile(rows, ATTN_ROWS, 1)
    tq = rps * GRID_W
    steps = rows // rps
    return pl.pallas_call(
        functools.partial(_attn_kernel, rows=rows, rows_per_step=rps),
        grid=(batch, pairs, steps),
        in_specs=[
            pl.BlockSpec((None, tq, LANES), lambda b, p, s: (p, b * steps + s, 0)),
            pl.BlockSpec((None, seq, LANES), lambda b, p, s: (pairs + p, b, 0)),
            pl.BlockSpec((None, seq, LANES), lambda b, p, s: (2 * pairs + p, b, 0)),
            _layer_block(layer, (None,) + bias.shape[2:], lambda b, p, s: (p, 0, 0, 0)),
        ],
        out_specs=pl.BlockSpec((tq, LANES), lambda b, p, s: (b * steps + s, p)),
        out_shape=jax.ShapeDtypeStruct((n, pairs * LANES), BF16),
        compiler_params=_params("arbitrary", "arbitrary", "arbitrary"),
        name="neighbourhood_attention",
    )(qkv, qkv, qkv, bias)


def _mix_kernel(attn_ref, cu_ref, ga_ref, gc_ref, x_ref, mod_ref, wa_ref, wc_ref, wo_ref, g_ref, b_ref,
                x1_ref, h2_ref, *, alpha):
    for start in range(0, x_ref.shape[0], MIX_ROWS):
        rows = slice(start, start + MIX_ROWS)
        y_a = jnp.dot(attn_ref[rows, :], wa_ref[...], preferred_element_type=F32)
        y_c = jnp.dot(cu_ref[rows, :], wc_ref[...], preferred_element_type=F32)
        y = (jax.nn.sigmoid(ga_ref[rows, :].astype(F32)) * y_a
             + jax.nn.sigmoid(gc_ref[rows, :].astype(F32)) * y_c)
        y = jnp.dot(y.astype(BF16), wo_ref[...], preferred_element_type=F32)
        x1 = _ln(alpha * x_ref[rows, :] + mod_ref[2:3, :] * y, g_ref[...], b_ref[...])
        x1_ref[rows, :] = x1
        h2_ref[rows, :] = (x1 * (1.0 + mod_ref[4:5, :]) + mod_ref[3:4, :]).astype(BF16)


def _mix(attn, cu, gates, x, mod, w_attn_proj, w_conv_proj, w_out, layer, ln_g, ln_b, seq, alpha):
    n, d = x.shape
    c = cu.shape[1]
    da = attn.shape[1]
    tm = _tile(seq, 2 * MIX_ROWS, MIX_ROWS)
    row = lambda i: (i, 0)
    return pl.pallas_call(
        functools.partial(_mix_kernel, alpha=alpha),
        grid=(n // tm,),
        in_specs=[
            pl.BlockSpec((tm, da), row),
            pl.BlockSpec((tm, c), row),
            pl.BlockSpec((tm, d), lambda i: (i, 0)),
            pl.BlockSpec((tm, d), lambda i: (i, 1)),
            pl.BlockSpec((tm, d), row),
            pl.BlockSpec((None, N_MOD, d), lambda i: (i * tm // seq, 0, 0)),
            _resident_layer(w_attn_proj, layer),
            _resident_layer(w_conv_proj, layer),
            _resident_layer(w_out, layer),
            _resident(ln_g.shape),
            _resident(ln_b.shape),
        ],
        out_specs=[pl.BlockSpec((tm, d), row), pl.BlockSpec((tm, d), row)],
        out_shape=[jax.ShapeDtypeStruct((n, d), F32), jax.ShapeDtypeStruct((n, d), BF16)],
        compiler_params=_params("arbitrary"),
        name="merge_out_projection",
    )(attn, cu, gates, gates, x, mod, w_attn_proj, w_conv_proj, w_out, ln_g, ln_b)


def _ffn_kernel(h_ref, hp_ref, hn_ref, x_ref, mod_ref, wa_ref, wb_ref, cw_ref, cb_ref, wd_ref, g_ref, b_ref,
                o_ref, h_ext, acc, *, tiles_per_seq, alpha):
    tm = h_ref.shape[0]
    j = pl.program_id(1)
    t = pl.program_id(0) % tiles_per_seq

    @pl.when(j == 0)
    def _():
        nxt = jnp.where(t == tiles_per_seq - 1, 0.0, hn_ref[...].astype(F32))
        prv = jnp.where(t == 0, 0.0, hp_ref[...].astype(F32))
        first_half = lax.broadcasted_iota(jnp.int32, nxt.shape, 0) < HALO // 2
        h_ext[0:tm, :] = h_ref[...]
        h_ext[tm:, :] = jnp.where(first_half, nxt, prv).astype(BF16)
        acc[...] = jnp.zeros_like(acc)

    ring = jnp.dot(h_ext[...], wa_ref[...], preferred_element_type=F32)
    gate = jnp.dot(h_ref[...], wb_ref[...], preferred_element_type=F32)
    before = pltpu.roll(ring, 1, 0)[0:tm]
    after = pltpu.roll(ring, ring.shape[0] - 1, 0)[0:tm]
    a = cb_ref[...] + cw_ref[0:1, :] * before + cw_ref[1:2, :] * ring[0:tm] + cw_ref[2:3, :] * after
    act = 0.5 * a * (1.0 + lax.erf(a * (2.0 ** -0.5))) * gate
    acc[...] += jnp.dot(act.astype(BF16), wd_ref[...], preferred_element_type=F32)

    @pl.when(j == pl.num_programs(1) - 1)
    def _():
        o_ref[...] = _ln(alpha * x_ref[...] + mod_ref[5:6, :] * acc[...], g_ref[...], b_ref[...])


def _ffn(h2, x1, mod, w_up_a, w_up_b, w_down, layer, conv_w, conv_b, ln_g, ln_b, seq, alpha):
    n, d = x1.shape
    fp = w_down.shape[1]
    tm = _tile(seq, TOKEN_TILE, HALO)
    tf = _tile(fp, FFN_SLAB, LANES)
    nf = fp // tf
    th = tm // HALO
    last_halo = n // HALO - 1
    row = lambda i, j: (i, 0)
    return pl.pallas_call(
        functools.partial(_ffn_kernel, tiles_per_seq=seq // tm, alpha=alpha),
        grid=(n // tm, nf),
        in_specs=[
            pl.BlockSpec((tm, d), row),
            pl.BlockSpec((HALO, d), lambda i, j: (jnp.maximum(i * th - 1, 0), 0)),
            pl.BlockSpec((HALO, d), lambda i, j: (jnp.minimum((i + 1) * th, last_halo), 0)),
            pl.BlockSpec((tm, d), row),
            pl.BlockSpec((None, N_MOD, d), lambda i, j: (i * tm // seq, 0, 0)),
            _layer_block(layer, (d, tf), lambda i, j: (0, j)),
            _layer_block(layer, (d, tf), lambda i, j: (0, j)),
            pl.BlockSpec((FFN_CONV_K, tf), lambda i, j: (0, j)),
            pl.BlockSpec((1, tf), lambda i, j: (0, j)),
            _layer_block(layer, (tf, d), lambda i, j: (j, 0)),
            _resident(ln_g.shape), _resident(ln_b.shape),
        ],
        out_specs=pl.BlockSpec((tm, d), row),
        out_shape=jax.ShapeDtypeStruct((n, d), F32),
        scratch_shapes=[pltpu.VMEM((tm + HALO, d), BF16),
                        pltpu.VMEM((tm, d), F32)],
        compiler_params=_params("arbitrary", "arbitrary"),
        name="conv_ffn",
    )(h2, h2, h2, x1, mod, w_up_a, w_up_b, conv_w, conv_b, w_down, ln_g, ln_b)


def _pad_to(w, width, axis):
    shape = list(w.shape)
    shape[axis] = width - w.shape[axis]
    return jnp.concatenate([w, jnp.zeros(shape, w.dtype)], axis=axis)


def _trunk(x, mods, ln_in, big, layers, alpha):
    batch, seq, d = x.shape
    x = x.reshape(batch * seq, d)
    d_attn, c = big["w_attn_proj"].shape[1], big["w_conv_proj"].shape[1]
    for l, p in enumerate(layers):
        mod = mods[l]
        if l == 0:
            x, h, u = _glu_projection(x, mod, big["w_in"], l, p["b_in"], seq, d_attn, c, ln_in=ln_in)
        else:
            h, u = _glu_projection(x, mod, big["w_in"], l, p["b_in"], seq, d_attn, c)
        qkv, gates = _in_projection(h, big["w_in"], l, p["b_in"], seq, d_attn, c)
        attn = _attention(qkv, big["attn_bias"], l, batch, seq)
        cu = _conv_branch(u, p["conv_w"], p["conv_b"], p["conv_ln_g"], p["conv_ln_b"], seq)
        x1, h2 = _mix(attn, cu, gates, x, mod, big["w_attn_proj"], big["w_conv_proj"], big["w_out"], l,
                      p["ln_mix_g"], p["ln_mix_b"], seq, alpha)
        x = _ffn(h2, x1, mod, big["w_up_a"], big["w_up_b"], big["w_down"], l,
                 p["ffn_conv_w"], p["ffn_conv_b"], p["ln_ffn_g"], p["ln_ffn_b"], seq, alpha)
    return x.reshape(batch, seq, d)


def kernel(x_prompt, x_sample, c_prompt, c_sample, ln_in_g, ln_in_b, w_mod, b_mod, w_in, b_in, na_rpb, w_attn_proj, conv_w, conv_b, conv_ln_g, conv_ln_b, w_conv_proj, w_out, ln_mix_g, ln_mix_b, w_up, ffn_conv_w, ffn_conv_b, w_down, ln_ffn_g, ln_ffn_b):
    depth, d, _ = w_mod.shape
    alpha = (2 * depth) ** 0.25
    nb_p, nb_s = c_prompt.shape[0], c_sample.shape[0]
    c_all = jnp.concatenate([c_prompt, c_sample], axis=0)
    mod_all = _modulation(c_all, w_mod, b_mod[:, None, :]).reshape(depth, nb_p + nb_s, N_MOD, d)
    mods_p = [mod_all[l, :nb_p] for l in range(depth)]
    mods_s = [mod_all[l, nb_p:] for l in range(depth)]

    d_ff = w_down.shape[1]
    fp = -(-d_ff // FFN_SLAB) * FFN_SLAB if d_ff > FFN_SLAB else -(-d_ff // LANES) * LANES
    w_up_bf16 = w_up.astype(BF16)
    big = dict(
        w_in=w_in.astype(BF16), w_attn_proj=w_attn_proj.astype(BF16), w_conv_proj=w_conv_proj.astype(BF16),
        w_out=w_out.astype(BF16),
        w_up_a=_pad_to(w_up_bf16[:, :, :d_ff], fp, 2), w_up_b=_pad_to(w_up_bf16[:, :, d_ff:], fp, 2),
        w_down=_pad_to(w_down.astype(BF16), fp, 1),
        attn_bias=_bias_table(na_rpb))

    row = lambda v: v[None, :]
    layers = []
    for l in range(depth):
        layers.append(dict(
            b_in=row(b_in[l]),
            conv_w=jnp.broadcast_to(conv_w[l][:, None, :], (conv_w.shape[1], SUBLANES, conv_w.shape[2])),
            conv_b=jnp.broadcast_to(conv_b[l][None, :], (SUBLANES, conv_b.shape[1])),
            conv_ln_g=row(conv_ln_g[l]), conv_ln_b=row(conv_ln_b[l]),
            ln_mix_g=row(ln_mix_g[l]), ln_mix_b=row(ln_mix_b[l]),
            ffn_conv_w=_pad_to(ffn_conv_w[l], fp, 1), ffn_conv_b=_pad_to(row(ffn_conv_b[l]), fp, 1),
            ln_ffn_g=row(ln_ffn_g[l]), ln_ffn_b=row(ln_ffn_b[l]),
        ))
    ln_in = (row(ln_in_g), row(ln_in_b))
    y_prompt = _trunk(x_prompt, mods_p, ln_in, big, layers, alpha)
    y_sample = _trunk(x_sample, mods_s, ln_in, big, layers, alpha)
    return (y_prompt, y_sample)
```
